```python
import math
import jax, jax.numpy as jnp
from jax import lax
import numpy as np

D_MODEL = 1024
BATCH = 4
SEQ = 4096
DEPTH = 2

CTX_LEN = 256
GRID_W = 64

S5_WIDTH = 512
S5_GROUP = 16
S5_GROUPS = S5_WIDTH // S5_GROUP
S5_STATE = 64
S5_DT_MIN = 1e-3
S5_DT_MAX = 1e-1
SC_WIDTH = 512
HY_WIDTH = 512
HY_BANDS = 16
HY_EMB = 1 + 2 * HY_BANDS
HY_FILTER_HIDDEN = 64
HY_FAST_DECAY = 0.3
HY_SLOW_DECAY = 1.5
HY_DECAY_TARGET = 1e-2
HY_DECAY_SHIFT = 0.05
HY_MAX_DECAY = math.log(HY_DECAY_TARGET) / HY_FAST_DECAY
HY_MIN_DECAY = math.log(HY_DECAY_TARGET) / HY_SLOW_DECAY

N_BRANCH = 3
D_FF = 4 * D_MODEL
N_MOD = 6
EPS = 1e-6

OFF_S5 = 0
OFF_SC = OFF_S5 + S5_WIDTH
OFF_HY = OFF_SC + 3 * SC_WIDTH
OFF_GATE = OFF_HY + 3 * HY_WIDTH
D_IN = OFF_GATE + N_BRANCH * D_MODEL

kernel_name = "hybrid_s5_shortconv_hyena_prefix_dit"

F32 = jnp.float32


def rmsnorm(x, g):
    xf = x.astype(F32)
    y = xf * lax.rsqrt(jnp.mean(xf * xf, axis=-1, keepdims=True) + EPS)
    return (y * g.astype(F32)).astype(x.dtype)


def modulate(h, shift, scale):
    return h * (1.0 + scale) + shift


def sincos_2d(rows, cols, dim):
    quarter = dim // 4
    omega = 1.0 / (10000.0 ** (jnp.arange(quarter, dtype=F32) / quarter))
    er = jnp.arange(rows, dtype=F32)[:, None] * omega[None]
    ec = jnp.arange(cols, dtype=F32)[:, None] * omega[None]
    er = jnp.concatenate([jnp.sin(er), jnp.cos(er)], axis=-1)
    ec = jnp.concatenate([jnp.sin(ec), jnp.cos(ec)], axis=-1)
    emb = jnp.concatenate([
        jnp.broadcast_to(er[:, None, :], (rows, cols, dim // 2)),
        jnp.broadcast_to(ec[None, :, :], (rows, cols, dim // 2))], axis=-1)
    return emb.reshape(rows * cols, dim)


def conv3(x, w, b=None):
    xp = jnp.pad(x, ((0, 0), (1, 1), (0, 0)))
    y = w[0] * xp[:, :-2] + w[1] * xp[:, 1:-1] + w[2] * xp[:, 2:]
    if b is not None:
        y = y + b
    return y


def s5_discretise(a_re, a_im, log_dt, b_re, b_im):
    a_re = a_re.astype(F32); a_im = a_im.astype(F32)
    dt = jnp.exp(log_dt.astype(F32))[:, None]
    mag = jnp.exp(a_re * dt)
    ang = a_im * dt
    abar_re = mag * jnp.cos(ang)
    abar_im = mag * jnp.sin(ang)
    den = a_re * a_re + a_im * a_im
    nr = abar_re - 1.0
    ni = abar_im
    f_re = (nr * a_re + ni * a_im) / den
    f_im = (ni * a_re - nr * a_im) / den
    b_re = b_re.astype(F32); b_im = b_im.astype(F32)
    bb_re = f_re[..., None] * b_re - f_im[..., None] * b_im
    bb_im = f_re[..., None] * b_im + f_im[..., None] * b_re
    return abar_re, abar_im, bb_re, bb_im


def _cscan_combine(e1, e2):
    a1r, a1i, b1r, b1i = e1
    a2r, a2i, b2r, b2i = e2
    return (a2r * a1r - a2i * a1i,
            a2r * a1i + a2i * a1r,
            a2r * b1r - a2i * b1i + b2r,
            a2r * b1i + a2i * b1r + b2i)


def s5_mixer(u, lp, init, readout):
    bsz, L, _ = u.shape
    uf = u.astype(F32)
    ug = uf.reshape(bsz, L, S5_GROUPS, S5_GROUP)
    states = []
    finals = []
    for k, rev in enumerate((False, True)):
        abr, abi, bbr, bbi = s5_discretise(lp["s5_a_re"][k], lp["s5_a_im"][k], lp["s5_log_dt"][k],
                                           lp["s5_b_re"][k], lp["s5_b_im"][k])
        bur = jnp.einsum("blgp,gnp->blgn", ug, bbr)
        bui = jnp.einsum("blgp,gnp->blgn", ug, bbi)
        if init is not None:
            s0r, s0i = init[k]
            edge = -1 if rev else 0
            bur = bur.at[:, edge].add(abr * s0r - abi * s0i)
            bui = bui.at[:, edge].add(abr * s0i + abi * s0r)
        ar = jnp.broadcast_to(abr, bur.shape)
        ai = jnp.broadcast_to(abi, bui.shape)
        _, _, sr, si = lax.associative_scan(_cscan_combine, (ar, ai, bur, bui), reverse=rev, axis=1)
        last = 0 if rev else -1
        finals.append((sr[:, last], si[:, last]))
        states.append((k, sr, si))
    if not readout:
        return None, finals
    y = uf * lp["s5_d"].astype(F32)
    for k, sr, si in states:
        cr = lp["s5_c_re"][k].astype(F32)
        ci = lp["s5_c_im"][k].astype(F32)
        yk = jnp.einsum("blgn,gpn->blgp", sr, cr) - jnp.einsum("blgn,gpn->blgp", si, ci)
        y = y + yk.reshape(bsz, L, S5_WIDTH)
    h = jax.nn.gelu(y).astype(u.dtype)
    a, g = jnp.split(h @ lp["s5_glu_w"], 2, axis=-1)
    return a * jax.nn.sigmoid(g), finals


def short_conv_branch(p, lp):
    sc_x, sc_b, sc_c = jnp.split(p[..., OFF_SC:OFF_HY], 3, axis=-1)
    return (sc_b * conv3(sc_c * sc_x, lp["sc_conv_w"])) @ lp["sc_out_w"]


def hyena_filter(L, lp):
    t = jnp.linspace(0.0, 1.0, L, dtype=F32)[:, None]
    ang = 2.0 * math.pi * jnp.arange(L, dtype=F32)[:, None] / L
    bands = jnp.linspace(1e-4, HY_BANDS - 1, HY_BANDS, dtype=F32)[None, :]
    z = jnp.concatenate([t, jnp.cos(bands * ang), -jnp.sin(bands * ang)], axis=-1)
    freq = lp["hy_f_freq"].astype(F32)
    h = jnp.sin(freq * (z @ lp["hy_f_w1"].astype(F32) + lp["hy_f_b1"].astype(F32)))
    h = jnp.sin(freq * (h @ lp["hy_f_w2"].astype(F32) + lp["hy_f_b2"].astype(F32)))
    h = jnp.sin(freq * (h @ lp["hy_f_w3"].astype(F32) + lp["hy_f_b3"].astype(F32)))
    h = h @ lp["hy_f_w4"].astype(F32)
    deltas = jnp.abs(jnp.linspace(HY_MIN_DECAY, HY_MAX_DECAY, HY_WIDTH, dtype=F32))
    window = jnp.exp(-t * deltas[None, :]) + HY_DECAY_SHIFT
    h_f = h[:, :HY_WIDTH] * window
    h_b = h[:, HY_WIDTH:] * window
    k = jnp.concatenate([h_f, jnp.zeros((1, HY_WIDTH), F32), h_b[:0:-1]], axis=0)
    return k / jnp.sum(jnp.abs(k), axis=0, keepdims=True)


def fft_long_conv(u, k, d):
    L = u.shape[1]
    uf = u.astype(F32)
    U = jnp.fft.rfft(uf, n=2 * L, axis=1)
    K = jnp.fft.rfft(k, n=2 * L, axis=0)
    y = jnp.fft.irfft(U * K[None], n=2 * L, axis=1)[:, :L]
    return y + uf * d.astype(F32)


def hyena_branch(p, lp):
    L = p.shape[1]
    q = conv3(p[..., OFF_HY:OFF_GATE], lp["hy_conv_w"], lp["hy_conv_b"])
    v, x1, x0 = jnp.split(q, 3, axis=-1)
    z = fft_long_conv(x1 * v, hyena_filter(L, lp), lp["hy_d"]).astype(p.dtype)
    return (x0 * z) @ lp["hy_out_w"]


def merge_branches(p, y_s5, lp):
    g = jax.nn.sigmoid(p[..., OFF_GATE:]).reshape(p.shape[:-1] + (N_BRANCH, D_MODEL))
    m = (g[..., 0, :] * y_s5
         + g[..., 1, :] * short_conv_branch(p, lp)
         + g[..., 2, :] * hyena_branch(p, lp))
    return m @ lp["out_w"]


def channel_mixer(h, lp):
    r = jax.nn.relu(h @ lp["mlp_w1"])
    return (r * r) @ lp["mlp_w2"]


def setup_inputs(seed: int = 0) -> dict:
    key = jax.random.key(seed)
    ks = iter(jax.random.split(key, 48))

    def nrm(shape, scale):
        return jax.random.normal(next(ks), shape, F32) * scale

    G, N, P, H = S5_GROUPS, S5_STATE, S5_GROUP, HY_FILTER_HIDDEN
    inp = {}
    inp["x"] = nrm((BATCH, SEQ, D_MODEL), 1.0)
    inp["c"] = nrm((BATCH, D_MODEL), 1.0)
    inp["ctx"] = nrm((BATCH, CTX_LEN, D_MODEL), 1.0)
    inp["c_ctx"] = nrm((D_MODEL,), 1.0)
    inp["ada_w"] = nrm((DEPTH, D_MODEL, N_MOD * D_MODEL), 0.5 * D_MODEL ** -0.5)
    inp["ada_b"] = nrm((DEPTH, N_MOD * D_MODEL), 0.02)
    inp["norm1_g"] = 1.0 + nrm((DEPTH, D_MODEL), 0.05)
    inp["norm2_g"] = 1.0 + nrm((DEPTH, D_MODEL), 0.05)
    inp["w_in"] = nrm((DEPTH, D_MODEL, D_IN), D_MODEL ** -0.5)
    inp["s5_a_re"] = -0.5 + nrm((DEPTH, 2, G, N), 0.01)
    inp["s5_a_im"] = math.pi * jnp.arange(N, dtype=F32) + nrm((DEPTH, 2, G, N), 0.01)
    inp["s5_log_dt"] = jax.random.uniform(next(ks), (DEPTH, 2, G), F32,
                                          math.log(S5_DT_MIN), math.log(S5_DT_MAX))
    inp["s5_b_re"] = nrm((DEPTH, 2, G, N, P), (2 * P) ** -0.5)
    inp["s5_b_im"] = nrm((DEPTH, 2, G, N, P), (2 * P) ** -0.5)
    inp["s5_c_re"] = nrm((DEPTH, 2, G, P, N), (2 * N) ** -0.5)
    inp["s5_c_im"] = nrm((DEPTH, 2, G, P, N), (2 * N) ** -0.5)
    inp["s5_d"] = nrm((DEPTH, S5_WIDTH), 1.0)
    inp["s5_glu_w"] = nrm((DEPTH, S5_WIDTH, 2 * D_MODEL), S5_WIDTH ** -0.5)
    inp["sc_conv_w"] = nrm((DEPTH, 3, SC_WIDTH), 3 ** -0.5)
    inp["sc_out_w"] = nrm((DEPTH, SC_WIDTH, D_MODEL), SC_WIDTH ** -0.5)
    inp["hy_conv_w"] = nrm((DEPTH, 3, 3 * HY_WIDTH), 3 ** -0.5)
    inp["hy_conv_b"] = nrm((DEPTH, 3 * HY_WIDTH), 0.02)
    inp["hy_f_w1"] = nrm((DEPTH, HY_EMB, H), HY_EMB ** -0.5)
    inp["hy_f_b1"] = nrm((DEPTH, H), 0.1)
    inp["hy_f_w2"] = nrm((DEPTH, H, H), H ** -0.5)
    inp["hy_f_b2"] = nrm((DEPTH, H), 0.1)
    inp["hy_f_w3"] = nrm((DEPTH, H, H), H ** -0.5)
    inp["hy_f_b3"] = nrm((DEPTH, H), 0.1)
    inp["hy_f_freq"] = 1.0 + nrm((DEPTH, H), 0.05)
    inp["hy_f_w4"] = nrm((DEPTH, H, 2 * HY_WIDTH), H ** -0.5)
    inp["hy_d"] = nrm((DEPTH, HY_WIDTH), 0.1)
    inp["hy_out_w"] = nrm((DEPTH, HY_WIDTH, D_MODEL), HY_WIDTH ** -0.5)
    inp["out_w"] = nrm((DEPTH, D_MODEL, D_MODEL), D_MODEL ** -0.5)
    inp["mlp_w1"] = nrm((DEPTH, D_MODEL, D_FF), D_MODEL ** -0.5)
    inp["mlp_w2"] = nrm((DEPTH, D_FF, D_MODEL), D_FF ** -0.5)
    inp["final_g"] = 1.0 + nrm((D_MODEL,), 0.05)
    return inp


def reference(x, c, ctx, c_ctx, ada_w, ada_b, norm1_g, norm2_g, w_in,
              s5_a_re, s5_a_im, s5_log_dt, s5_b_re, s5_b_im, s5_c_re, s5_c_im, s5_d, s5_glu_w,
              sc_conv_w, sc_out_w, hy_conv_w, hy_conv_b,
              hy_f_w1, hy_f_b1, hy_f_w2, hy_f_b2, hy_f_w3, hy_f_b3, hy_f_freq, hy_f_w4,
              hy_d, hy_out_w, out_w, mlp_w1, mlp_w2, final_g):
    L = x.shape[1]
    rows = L // GRID_W
    x = x + sincos_2d(rows, GRID_W, D_MODEL).astype(x.dtype)[None]
    xc = ctx
    sc_vec = jax.nn.silu(c)
    sc_ctx_vec = jax.nn.silu(c_ctx)
    for l in range(DEPTH):
        last = l == DEPTH - 1
        lp = {
            "s5_a_re": s5_a_re[l], "s5_a_im": s5_a_im[l], "s5_log_dt": s5_log_dt[l],
            "s5_b_re": s5_b_re[l], "s5_b_im": s5_b_im[l], "s5_c_re": s5_c_re[l],
            "s5_c_im": s5_c_im[l], "s5_d": s5_d[l], "s5_glu_w": s5_glu_w[l],
            "sc_conv_w": sc_conv_w[l], "sc_out_w": sc_out_w[l],
            "hy_conv_w": hy_conv_w[l], "hy_conv_b": hy_conv_b[l],
            "hy_f_w1": hy_f_w1[l], "hy_f_b1": hy_f_b1[l], "hy_f_w2": hy_f_w2[l],
            "hy_f_b2": hy_f_b2[l], "hy_f_w3": hy_f_w3[l], "hy_f_b3": hy_f_b3[l],
            "hy_f_freq": hy_f_freq[l], "hy_f_w4": hy_f_w4[l], "hy_d": hy_d[l],
            "hy_out_w": hy_out_w[l], "out_w": out_w[l],
            "mlp_w1": mlp_w1[l], "mlp_w2": mlp_w2[l],
        }
        mod = (sc_vec @ ada_w[l] + ada_b[l])[:, None, :]
        mod_c = sc_ctx_vec @ ada_w[l] + ada_b[l]
        sh1, scl1, g1, sh2, scl2, g2 = jnp.split(mod, N_MOD, axis=-1)
        csh1, cscl1, cg1, csh2, cscl2, cg2 = jnp.split(mod_c, N_MOD, axis=-1)
        w_in_l = w_in[l]

        hc = modulate(rmsnorm(xc, norm1_g[l]), csh1, cscl1)
        pc = hc @ (w_in_l[:, OFF_S5:OFF_S5 + S5_WIDTH] if last else w_in_l)
        yc_s5, ctx_final = s5_mixer(pc[..., OFF_S5:OFF_S5 + S5_WIDTH], lp, None, not last)

        h = modulate(rmsnorm(x, norm1_g[l]), sh1, scl1)
        p = h @ w_in_l
        y_s5, _ = s5_mixer(p[..., OFF_S5:OFF_S5 + S5_WIDTH], lp, ctx_final, True)
        x = x + g1 * merge_branches(p, y_s5, lp)
        x = x + g2 * channel_mixer(modulate(rmsnorm(x, norm2_g[l]), sh2, scl2), lp)

        if not last:
            xc = xc + cg1 * merge_branches(pc, yc_s5, lp)
            xc = xc + cg2 * channel_mixer(modulate(rmsnorm(xc, norm2_g[l]), csh2, cscl2), lp)
    return rmsnorm(x, final_g)
```

```python
import functools
import math

import numpy as np
import jax
import jax.numpy as jnp
from jax import lax
from jax.experimental import pallas as pl
from jax.experimental.pallas import tpu as pltpu

F32 = jnp.float32
BF16 = jnp.bfloat16
HIGHEST = lax.Precision.HIGHEST

D_MODEL = 1024
DEPTH = 2
GRID_W = 64
S5_WIDTH = 512
S5_GROUP = 16
S5_GROUPS = S5_WIDTH // S5_GROUP
S5_STATE = 64
SC_WIDTH = 512
HY_WIDTH = 512
HY_BANDS = 16
HY_EMB = 1 + 2 * HY_BANDS
HY_FAST_DECAY = 0.3
HY_SLOW_DECAY = 1.5
HY_DECAY_TARGET = 1e-2
HY_DECAY_SHIFT = 0.05
HY_MAX_DECAY = math.log(HY_DECAY_TARGET) / HY_FAST_DECAY
HY_MIN_DECAY = math.log(HY_DECAY_TARGET) / HY_SLOW_DECAY
N_BRANCH = 3
D_FF = 4 * D_MODEL
N_MOD = 6
EPS = 1e-6
OFF_S5 = 0
OFF_SC = OFF_S5 + S5_WIDTH
OFF_HY = OFF_SC + 3 * SC_WIDTH
OFF_GATE = OFF_HY + 3 * HY_WIDTH
D_IN = OFF_GATE + N_BRANCH * D_MODEL

LANES = 128
SUBLANES = 8
S5_T = 8
S5_BLOCKS = S5_WIDTH // LANES
S5_HALF = 256
EMB_PAD = 40
MOD_ROWS = 8
CTX_ROW = 4


def _cparams(sem, vmem_mb):
    return pltpu.CompilerParams(dimension_semantics=sem, vmem_limit_bytes=vmem_mb * 1024 * 1024)


def _bdot(a, b):
    return jnp.dot(a.astype(BF16), b.astype(BF16), preferred_element_type=F32)


def _hdot(a, b):
    return jnp.dot(a, b, precision=HIGHEST, preferred_element_type=F32)


def _ada_kernel(c_ref, w_ref, b_ref, o_ref):
    c = c_ref[...]
    s = c * jax.nn.sigmoid(c)
    o_ref[...] = _bdot(s, w_ref[...]) + b_ref[...]


def ada_mod(cc, ada_w, ada_b):
    n = N_MOD * D_MODEL
    tn = 1536
    return pl.pallas_call(
        _ada_kernel,
        grid=(DEPTH, n // tn),
        in_specs=[
            pl.BlockSpec((MOD_ROWS, D_MODEL), lambda l, j: (0, 0)),
            pl.BlockSpec((None, D_MODEL, tn), lambda l, j: (l, 0, j)),
            pl.BlockSpec((None, 1, tn), lambda l, j: (l, 0, j)),
        ],
        out_specs=pl.BlockSpec((None, MOD_ROWS, tn), lambda l, j: (l, 0, j)),
        out_shape=jax.ShapeDtypeStruct((DEPTH, MOD_ROWS, n), F32),
        compiler_params=_cparams(("parallel", "parallel"), 40),
        name="ada_mod",
    )(cc, ada_w, ada_b.reshape(DEPTH, 1, n))


def _mod_spec(chunk, ctx):
    if ctx:
        return pl.BlockSpec((None, 1, D_MODEL), lambda b, *_: (CTX_ROW, 0, chunk))
    return pl.BlockSpec((None, 1, D_MODEL), lambda b, *_: (b, 0, chunk))


def _norm_mod(x, g, sh, sc):
    y = x * lax.rsqrt(jnp.mean(x * x, axis=-1, keepdims=True) + EPS)
    return (y * g) * (1.0 + sc) + sh


def _norm_mm_kernel(x_ref, g_ref, sh_ref, sc_ref, w_ref, o_ref, h_ref):
    @pl.when(pl.program_id(2) == 0)
    def _():
        h_ref[...] = _norm_mod(x_ref[...], g_ref[...], sh_ref[...], sc_ref[...]).astype(BF16)

    o_ref[...] = jnp.dot(h_ref[...], w_ref[...], preferred_element_type=F32)


def norm_mm(x, g, mod3, w, ctx):
    bsz, L, _ = x.shape
    n = w.shape[1]
    tl = min(L, 1024)
    tn = n // 4 if n % (4 * LANES) == 0 and n > 2048 else n
    return pl.pallas_call(
        _norm_mm_kernel,
        grid=(bsz, L // tl, n // tn),
        in_specs=[
            pl.BlockSpec((None, tl, D_MODEL), lambda b, i, j: (b, i, 0)),
            pl.BlockSpec((1, D_MODEL), lambda b, i, j: (0, 0)),
            _mod_spec(0, ctx),
            _mod_spec(1, ctx),
            pl.BlockSpec((D_MODEL, tn), lambda b, i, j: (0, j)),
        ],
        out_specs=pl.BlockSpec((None, tl, tn), lambda b, i, j: (b, i, j)),
        out_shape=jax.ShapeDtypeStruct((bsz, L, n), F32),
        scratch_shapes=[pltpu.VMEM((tl, D_MODEL), BF16)],
        compiler_params=_cparams(("parallel", "parallel", "arbitrary"), 48),
        name="norm_mm",
    )(x, g.reshape(1, D_MODEL), mod3, mod3, w)


def _s5_operators(a_re, a_im, log_dt, b_re, b_im, c_re, c_im, d_skip):
    T = S5_T
    G, N, P = S5_GROUPS, S5_STATE, S5_GROUP
    taus = jnp.arange(T + 1, dtype=F32)[:, None, None]
    ops = []
    for k in range(2):
        ar, ai = a_re[k].astype(F32), a_im[k].astype(F32)
        dt = jnp.exp(log_dt[k].astype(F32))[:, None]
        mag = jnp.exp(ar * dt * taus)
        ang = ai * dt * taus
        pr, pi = mag * jnp.cos(ang), mag * jnp.sin(ang)
        den = ar * ar + ai * ai
        nr, ni = pr[1] - 1.0, pi[1]
        f_re = (nr * ar + ni * ai) / den
        f_im = (ni * ar - nr * ai) / den
        br, bi = b_re[k].astype(F32), b_im[k].astype(F32)
        bbr = f_re[..., None] * br - f_im[..., None] * bi
        bbi = f_re[..., None] * bi + f_im[..., None] * br
        er = pr[..., None] * bbr - pi[..., None] * bbi
        ei = pr[..., None] * bbi + pi[..., None] * bbr
        cr, ci = c_re[k].astype(F32), c_im[k].astype(F32)
        kk = (jnp.einsum("gqn,tgnp->tgqp", cr, er[:T], precision=HIGHEST)
              - jnp.einsum("gqn,tgnp->tgqp", ci, ei[:T], precision=HIGHEST))
        car = cr[None] * pr[:, :, None, :] - ci[None] * pi[:, :, None, :]
        cai = cr[None] * pi[:, :, None, :] + ci[None] * pr[:, :, None, :]
        ops.append((pr, pi, er, ei, kk, car, cai))

    eye8 = jnp.eye(8, dtype=F32)

    def block_diag_k(kk):
        k5 = kk.reshape(T, S5_BLOCKS, 8, P, P)
        out = jnp.einsum("tbgqp,gh->tbgphq", k5, eye8)
        return out.reshape(T, S5_BLOCKS, LANES, LANES)

    kbf, kbr = block_diag_k(ops[0][4]), block_diag_k(ops[1][4])
    j_in = jnp.arange(T)[:, None]
    j_out = jnp.arange(T)[None, :]
    tau = j_out - j_in
    mf = jnp.where((tau >= 0)[..., None, None, None], kbf[jnp.clip(tau, 0, T - 1)], 0.0)
    mr = jnp.where((tau <= 0)[..., None, None, None], kbr[jnp.clip(-tau, 0, T - 1)], 0.0)
    dsk = d_skip.astype(F32).reshape(S5_BLOCKS, LANES)
    dmat = jnp.einsum("bc,cd->bcd", dsk, jnp.eye(LANES, dtype=F32))
    md = jnp.where((tau == 0)[..., None, None, None], dmat[None, None], 0.0)
    m_all = mf + mr + md
    m_op = m_all.transpose(2, 0, 3, 1, 4).reshape(S5_BLOCKS, T * LANES, T * LANES)

    oh = (jnp.arange(8)[None, None, :] == (jnp.arange(2)[:, None, None] * 4 + jnp.arange(4)[None, :, None])).astype(F32)

    def b_mat(er, ei, sel):
        def one(e):
            e6 = e[sel].reshape(T, S5_BLOCKS, 2, 4, N, P)
            return jnp.einsum("jbhlnp,hlg->bhjgpln", e6, oh)
        both = jnp.stack([one(er), one(ei)], axis=5)
        return both.reshape(S5_BLOCKS, 2, T * LANES, 2 * S5_HALF)

    def c_mat(car, cai, sel):
        def one(c):
            c6 = c[sel].reshape(T, S5_BLOCKS, 2, 4, P, N)
            return jnp.einsum("jbhlqn,hlg->bhlnjgq", c6, oh)
        both = jnp.stack([one(car), -one(cai)], axis=2)
        return both.reshape(S5_BLOCKS, 4 * S5_HALF, T * LANES)

    jj = jnp.arange(T)
    b_f = b_mat(ops[0][2], ops[0][3], T - 1 - jj)
    b_r = b_mat(ops[1][2], ops[1][3], jj)
    c_f = c_mat(ops[0][5], ops[0][6], jj + 1)
    c_r = c_mat(ops[1][5], ops[1][6], T - jj)

    def at(pr, pi):
        r = pr[T].reshape(S5_BLOCKS, 2, 1, S5_HALF)
        i = pi[T].reshape(S5_BLOCKS, 2, 1, S5_HALF)
        return jnp.concatenate([r, i], axis=-1)

    return dict(m=m_op.astype(BF16), b_f=b_f.astype(BF16), b_r=b_r.astype(BF16),
                c_f=c_f.astype(BF16), c_r=c_r.astype(BF16),
                at_f=at(ops[0][0], ops[0][1]), at_r=at(ops[1][0], ops[1][1]))


def _cmul_add(a, s, d):
    h = S5_HALF
    ar, ai = a[:, :h], a[:, h:]
    sr, si = s[:, :h], s[:, h:]
    return jnp.concatenate([ar * sr - ai * si, ar * si + ai * sr], axis=1) + d


def _s5_scan_kernel(x_ref, bf_ref, br_ref, atf_ref, atr_ref, if_ref, ir_ref,
                    s_ref, r_ref, ff_ref, fr_ref, *, rows):
    rc = min(rows, 256)
    for r0 in range(0, rows, rc):
        xs = x_ref[r0:r0 + rc, :]
        s_ref[r0:r0 + rc, :] = jnp.dot(xs, bf_ref[...], preferred_element_type=F32)
        r_ref[r0:r0 + rc, :] = jnp.dot(xs, br_ref[...], preferred_element_type=F32)

    ntile = rows // SUBLANES
    atf = atf_ref[...]
    atr = atr_ref[...]
    low = lax.broadcasted_iota(jnp.int32, (SUBLANES, 2 * S5_HALF), 0) < 4

    def body(i, carry):
        s, r = carry
        fo = pl.multiple_of(i * SUBLANES, SUBLANES)
        d = s_ref[pl.ds(fo, SUBLANES), :]
        t1 = pltpu.roll(_cmul_add(atf, s, d), 4, 0)
        t2 = _cmul_add(atf, t1, d)
        s_ref[pl.ds(fo, SUBLANES), :] = jnp.where(low, s, t1)
        ro = pl.multiple_of((ntile - 1 - i) * SUBLANES, SUBLANES)
        e = r_ref[pl.ds(ro, SUBLANES), :]
        u1 = pltpu.roll(_cmul_add(atr, r, e), 4, 0)
        u2 = _cmul_add(atr, u1, e)
        r_ref[pl.ds(ro, SUBLANES), :] = jnp.where(low, u1, r)
        return pltpu.roll(t2, 4, 0), pltpu.roll(u2, 4, 0)

    s0 = if_ref[...]
    r0 = pltpu.roll(ir_ref[...], 4, 0)
    s, r = lax.fori_loop(0, ntile, body, (s0, r0))
    ff_ref[...] = s
    fr_ref[...] = pltpu.roll(r, 4, 0)


def s5_scan(x4, ops, init_f, init_r):
    nb, rows, tk = x4.shape
    w = 2 * S5_HALF
    kern = functools.partial(_s5_scan_kernel, rows=rows)
    st_spec = pl.BlockSpec((None, None, SUBLANES, w), lambda b, h: (b, h, 0, 0))
    out_shape = (
        jax.ShapeDtypeStruct((nb, rows, 2 * w), F32),
        jax.ShapeDtypeStruct((nb, rows, 2 * w), F32),
        jax.ShapeDtypeStruct((nb, 2, SUBLANES, w), F32),
        jax.ShapeDtypeStruct((nb, 2, SUBLANES, w), F32),
    )
    return pl.pallas_call(
        kern,
        grid=(nb, 2),
        in_specs=[
            pl.BlockSpec((None, rows, tk), lambda b, h: (b, 0, 0)),
            pl.BlockSpec((None, None, tk, w), lambda b, h: (b, h, 0, 0)),
            pl.BlockSpec((None, None, tk, w), lambda b, h: (b, h, 0, 0)),
            pl.BlockSpec((None, None, 1, w), lambda b, h: (b, h, 0, 0)),
            pl.BlockSpec((None, None, 1, w), lambda b, h: (b, h, 0, 0)),
            st_spec, st_spec,
        ],
        out_specs=(
            pl.BlockSpec((None, rows, w), lambda b, h: (b, 0, h)),
            pl.BlockSpec((None, rows, w), lambda b, h: (b, 0, h)),
            st_spec, st_spec,
        ),
        out_shape=out_shape,
        compiler_params=_cparams(("parallel", "parallel"), 48),
        name="s5_scan",
    )(x4, ops["b_f"], ops["b_r"], ops["at_f"], ops["at_r"], init_f, init_r)


def _s5_out_kernel(x_ref, s_ref, r_ref, m_ref, cf_ref, cr_ref, o_ref):
    acc = jnp.dot(x_ref[...], m_ref[...], preferred_element_type=F32)
    acc += jnp.dot(s_ref[...].astype(BF16), cf_ref[...], preferred_element_type=F32)
    acc += jnp.dot(r_ref[...].astype(BF16), cr_ref[...], preferred_element_type=F32)
    o_ref[...] = acc


def s5_out(x4, s_st, r_st, ops):
    nb, rows, tk = x4.shape
    tr = min(rows, 512)
    w = 4 * S5_HALF
    return pl.pallas_call(
        _s5_out_kernel,
        grid=(nb, rows // tr),
        in_specs=[
            pl.BlockSpec((None, tr, tk), lambda b, i: (b, i, 0)),
            pl.BlockSpec((None, tr, w), lambda b, i: (b, i, 0)),
            pl.BlockSpec((None, tr, w), lambda b, i: (b, i, 0)),
            pl.BlockSpec((None, tk, tk), lambda b, i: (b, 0, 0)),
            pl.BlockSpec((None, w, tk), lambda b, i: (b, 0, 0)),
            pl.BlockSpec((None, w, tk), lambda b, i: (b, 0, 0)),
        ],
        out_specs=pl.BlockSpec((None, tr, tk), lambda b, i: (b, i, 0)),
        out_shape=jax.ShapeDtypeStruct((nb, rows, tk), F32),
        compiler_params=_cparams(("parallel", "parallel"), 48),
        name="s5_out",
    )(x4, s_st, r_st, ops["m"], ops["c_f"], ops["c_r"])


def _to_chunk_rows(u):
    bsz, L, _ = u.shape
    nc = L // S5_T
    x = u.astype(BF16).reshape(bsz, nc, S5_T, S5_BLOCKS, LANES)
    return x.transpose(3, 1, 0, 2, 4).reshape(S5_BLOCKS, nc * bsz, S5_T * LANES)


def _from_chunk_rows(y4, bsz):
    nb, rows, _ = y4.shape
    nc = rows // bsz
    y = y4.reshape(nb, nc, bsz, S5_T, LANES).transpose(2, 1, 3, 0, 4)
    return y.reshape(bsz, nc * S5_T, S5_WIDTH)


def s5_layer(u, ops, init, readout):
    bsz = u.shape[0]
    x4 = _to_chunk_rows(u)
    if init is None:
        z = jnp.zeros((S5_BLOCKS, 2, SUBLANES, 2 * S5_HALF), F32)
        init = (z, z)
    s_st, r_st, fin_f, fin_r = s5_scan(x4, ops, init[0], init[1])
    y = _from_chunk_rows(s5_out(x4, s_st, r_st, ops), bsz) if readout else None
    return y, (fin_f, fin_r)


def _conv3(x, prev_row, next_row, w, first, last):
    n = x.shape[0]
    rows = lax.broadcasted_iota(jnp.int32, x.shape, 0)
    pz = jnp.where(first, 0.0, prev_row)
    nz = jnp.where(last, 0.0, next_row)
    xp = jnp.where(rows == 0, pz, pltpu.roll(x, 1, 0))
    xn = jnp.where(rows == n - 1, nz, pltpu.roll(x, n - 1, 0))
    return w[0:1, :] * xp + w[1:2, :] * x + w[2:3, :] * xn


def _halo_specs(tl, L, width, col):
    tb = tl // SUBLANES
    nb = L // SUBLANES
    cur = pl.BlockSpec((None, tl, width), lambda b, i: (b, i, col))
    prv = pl.BlockSpec((None, SUBLANES, width), lambda b, i: (b, jnp.maximum(i * tb - 1, 0), col))
    nxt = pl.BlockSpec((None, SUBLANES, width), lambda b, i: (b, jnp.minimum((i + 1) * tb, nb - 1), col))
    return [cur, prv, nxt]


def _hy_pre_kernel(v_ref, vp_ref, vn_ref, a_ref, ap_ref, an_ref, z_ref, zp_ref, zn_ref,
                   w_ref, b_ref, m_ref, x0_ref):
    i = pl.program_id(1)
    first = i == 0
    last = i == pl.num_programs(1) - 1
    w = w_ref[...]
    b = b_ref[...]

    def part(c, p, n, k):
        sl = slice(k * HY_WIDTH, (k + 1) * HY_WIDTH)
        return _conv3(c[...], p[SUBLANES - 1:SUBLANES, :], n[0:1, :], w[:, sl], first, last) + b[:, sl]

    v = part(v_ref, vp_ref, vn_ref, 0)
    x1 = part(a_ref, ap_ref, an_ref, 1)
    m_ref[...] = x1 * v
    x0_ref[...] = part(z_ref, zp_ref, zn_ref, 2)


def hy_pre(p, conv_w, conv_b):
    bsz, L, _ = p.shape
    tl = min(L, 512)
    c0 = OFF_HY // HY_WIDTH
    specs = []
    for k in range(3):
        specs += _halo_specs(tl, L, HY_WIDTH, c0 + k)
    specs += [pl.BlockSpec((3, 3 * HY_WIDTH), lambda b, i: (0, 0)),
              pl.BlockSpec((1, 3 * HY_WIDTH), lambda b, i: (0, 0))]
    o_spec = pl.BlockSpec((None, tl, HY_WIDTH), lambda b, i: (b, i, 0))
    shp = jax.ShapeDtypeStruct((bsz, L, HY_WIDTH), F32)
    return pl.pallas_call(
        _hy_pre_kernel,
        grid=(bsz, L // tl),
        in_specs=specs,
        out_specs=(o_spec, o_spec),
        out_shape=(shp, shp),
        compiler_params=_cparams(("parallel", "parallel"), 32),
        name="hy_pre",
    )(*([p] * 9), conv_w, conv_b.reshape(1, -1))


def _hy_filter_kernel(z_ref, w1_ref, b1_ref, w2_ref, b2_ref, w3_ref, b3_ref, fr_ref, w4_ref,
                      dl_ref, hf_ref, hb_ref, nrm_ref, *, tl):
    i = pl.program_id(0)
    z = z_ref[...]
    fr = fr_ref[...]
    h = jnp.sin(fr * (_hdot(z, w1_ref[...]) + b1_ref[...]))
    h = jnp.sin(fr * (_hdot(h, w2_ref[...]) + b2_ref[...]))
    h = jnp.sin(fr * (_hdot(h, w3_ref[...]) + b3_ref[...]))
    h = _hdot(h, w4_ref[...])
    t = z[:, 0:1]
    window = jnp.exp(-t * dl_ref[...]) + HY_DECAY_SHIFT
    hf = h[:, :HY_WIDTH] * window
    rows = lax.broadcasted_iota(jnp.int32, (tl, HY_WIDTH), 0) + i * tl
    hb = jnp.where(rows == 0, 0.0, h[:, HY_WIDTH:] * window)
    hf_ref[...] = hf
    hb_ref[...] = hb
    part = jnp.sum(jnp.abs(hf) + jnp.abs(hb), axis=0, keepdims=True)

    @pl.when(i == 0)
    def _():
        nrm_ref[...] = jnp.zeros_like(nrm_ref)

    nrm_ref[...] += part


def hy_filter_taps(L, lp):
    t = jnp.linspace(0.0, 1.0, L, dtype=F32)[:, None]
    ang = 2.0 * math.pi * jnp.arange(L, dtype=F32)[:, None] / L
    bands = jnp.linspace(1e-4, HY_BANDS - 1, HY_BANDS, dtype=F32)[None, :]
    z = jnp.concatenate([t, jnp.cos(bands * ang), -jnp.sin(bands * ang),
                         jnp.zeros((L, EMB_PAD - HY_EMB), F32)], axis=-1)
    w1 = jnp.concatenate([lp["hy_f_w1"], jnp.zeros((EMB_PAD - HY_EMB, lp["hy_f_w1"].shape[1]), F32)], axis=0)
    deltas = jnp.abs(jnp.linspace(HY_MIN_DECAY, HY_MAX_DECAY, HY_WIDTH, dtype=F32))[None, :]
    tl = min(L, 512)
    hid = w1.shape[1]
    full = lambda shape: pl.BlockSpec(shape, lambda i: (0,) * len(shape))
    o_spec = pl.BlockSpec((tl, HY_WIDTH), lambda i: (i, 0))
    shp = jax.ShapeDtypeStruct((L, HY_WIDTH), F32)
    hf, hb, nrm = pl.pallas_call(
        functools.partial(_hy_filter_kernel, tl=tl),
        grid=(L // tl,),
        in_specs=[pl.BlockSpec((tl, EMB_PAD), lambda i: (i, 0)),
                  full((EMB_PAD, hid)), full((1, hid)), full((hid, hid)), full((1, hid)),
                  full((hid, hid)), full((1, hid)), full((1, hid)), full((hid, 2 * HY_WIDTH)),
                  full((1, HY_WIDTH))],
        out_specs=(o_spec, o_spec, pl.BlockSpec((1, HY_WIDTH), lambda i: (0, 0))),
        out_shape=(shp, shp, jax.ShapeDtypeStruct((1, HY_WIDTH), F32)),
        compiler_params=_cparams(("arbitrary",), 32),
        name="hy_filter_taps",
    )(z, w1, lp["hy_f_b1"].reshape(1, -1), lp["hy_f_w2"], lp["hy_f_b2"].reshape(1, -1),
      lp["hy_f_w3"], lp["hy_f_b3"].reshape(1, -1), lp["hy_f_freq"].reshape(1, -1), lp["hy_f_w4"], deltas)
    return jnp.stack([hf, hb], axis=0), nrm


def _dft_plan(L):
    n = 2 * L
    n2 = 64 if L >= 4096 else 32
    n1 = n // n2
    return n, n1, n2


@functools.lru_cache(maxsize=None)
def _dft_consts(L):
    n, n1, n2 = _dft_plan(L)
    k1 = np.arange(n1)[:, None]
    m1 = np.arange(n1 // 2)[None, :]
    th = 2.0 * np.pi * k1 * m1 / n1
    c1 = np.cos(th)
    s1 = -np.sin(th)
    c1t = np.cos(th).T / n
    s1t = np.sin(th).T / n
    k2 = np.arange(n2)[:, None]
    m2 = np.arange(n2)[None, :]
    kk1 = np.arange(n1)[:, None, None]
    ph = -2.0 * np.pi * (k2 * m2 / n2)[None] - 2.0 * np.pi * kk1 * m2[None] / n
    gr, gi = np.cos(ph), np.sin(ph)
    ghr, ghi = gr.transpose(0, 2, 1), -gi.transpose(0, 2, 1)
    f = lambda a: np.asarray(a, np.float32)
    return dict(c1=f(c1), s1=f(s1), c1t=f(c1t), s1t=f(s1t), gr=f(gr), gi=f(gi), ghr=f(ghr), ghi=f(ghi))


def _dft1_kernel(x_ref, c_ref, s_ref, ar_ref, ai_ref):
    x = x_ref[...]
    ar_ref[...] = _hdot(c_ref[...], x)
    ai_ref[...] = _hdot(s_ref[...], x)


def dft_stage1(x, L):
    n, n1, n2 = _dft_plan(L)
    cs = _dft_consts(L)
    bsz, _, cols = x.shape
    tc = min(cols, 4096)
    o_spec = pl.BlockSpec((None, n1, tc), lambda b, j: (b, 0, j))
    shp = jax.ShapeDtypeStruct((bsz, n1, cols), F32)
    return pl.pallas_call(
        _dft1_kernel,
        grid=(bsz, cols // tc),
        in_specs=[pl.BlockSpec((None, n1 // 2, tc), lambda b, j: (b, 0, j)),
                  pl.BlockSpec((n1, n1 // 2), lambda b, j: (0, 0)),
                  pl.BlockSpec((n1, n1 // 2), lambda b, j: (0, 0))],
        out_specs=(o_spec, o_spec),
        out_shape=(shp, shp),
        compiler_params=_cparams(("parallel", "parallel"), 32),
        name="dft_stage1",
    )(x, cs["c1"], cs["s1"])


def _cdot(gr, gi, xr, xi):
    w = xr.shape[1]
    x = jnp.concatenate([xr, xi], axis=1)
    a = _hdot(gr, x)
    b = _hdot(gi, x)
    return a[:, :w] - b[:, w:], a[:, w:] + b[:, :w]


def _dft2_filter_kernel(ar_ref, ai_ref, gr_ref, gi_ref, inv_ref, kr_ref, ki_ref, *, kb, n2):
    inv = inv_ref[...]
    for k in range(kb):
        sl = slice(k * n2, (k + 1) * n2)
        fr, fi = _cdot(gr_ref[k], gi_ref[k], ar_ref[0, sl, :], ai_ref[0, sl, :])
        br, bi = _cdot(gr_ref[k], gi_ref[k], ar_ref[1, sl, :], ai_ref[1, sl, :])
        kr_ref[sl, :] = (fr + br) * inv
        ki_ref[sl, :] = (fi - bi) * inv


def dft_stage2_filter(ar, ai, inv_norm, L):
    n, n1, n2 = _dft_plan(L)
    cs = _dft_consts(L)
    kb = 8
    w = ar.shape[-1]
    a_spec = pl.BlockSpec((2, kb * n2, w), lambda j: (0, j, 0))
    g_spec = pl.BlockSpec((kb, n2, n2), lambda j: (j, 0, 0))
    o_spec = pl.BlockSpec((kb * n2, w), lambda j: (j, 0))
    shp = jax.ShapeDtypeStruct((n, w), F32)
    return pl.pallas_call(
        functools.partial(_dft2_filter_kernel, kb=kb, n2=n2),
        grid=(n1 // kb,),
        in_specs=[a_spec, a_spec, g_spec, g_spec, pl.BlockSpec((1, w), lambda j: (0, 0))],
        out_specs=(o_spec, o_spec),
        out_shape=(shp, shp),
        compiler_params=_cparams(("parallel",), 32),
        name="dft_stage2_filter",
    )(ar, ai, cs["gr"], cs["gi"], inv_norm)


def _dft2_conv_kernel(ar_ref, ai_ref, kr_ref, ki_ref, gr_ref, gi_ref, ghr_ref, ghi_ref,
                      br_ref, bi_ref, *, kb, n2):
    for k in range(kb):
        sl = slice(k * n2, (k + 1) * n2)
        yr, yi = _cdot(gr_ref[k], gi_ref[k], ar_ref[sl, :], ai_ref[sl, :])
        kr, ki = kr_ref[sl, :], ki_ref[sl, :]
        pr = yr * kr - yi * ki
        pi = yr * ki + yi * kr
        zr, zi = _cdot(ghr_ref[k], ghi_ref[k], pr, pi)
        br_ref[sl, :] = zr
        bi_ref[sl, :] = zi


def dft_stage2_conv(ar, ai, kr, ki, L):
    n, n1, n2 = _dft_plan(L)
    cs = _dft_consts(L)
    kb = 8
    bsz, _, w = ar.shape
    a_spec = pl.BlockSpec((None, kb * n2, w), lambda j, b: (b, j, 0))
    k_spec = pl.BlockSpec((kb * n2, w), lambda j, b: (j, 0))
    g_spec = pl.BlockSpec((kb, n2, n2), lambda j, b: (j, 0, 0))
    shp = jax.ShapeDtypeStruct((bsz, n, w), F32)
    return pl.pallas_call(
        functools.partial(_dft2_conv_kernel, kb=kb, n2=n2),
        grid=(n1 // kb, bsz),
        in_specs=[a_spec, a_spec, k_spec, k_spec, g_spec, g_spec, g_spec, g_spec],
        out_specs=(a_spec, a_spec),
        out_shape=(shp, shp),
        compiler_params=_cparams(("parallel", "parallel"), 32),
        name="dft_stage2_conv",
    )(ar, ai, kr, ki, cs["gr"], cs["gi"], cs["ghr"], cs["ghi"])


def _idft1_kernel(br_ref, bi_ref, m_ref, x0_ref, d_ref, c_ref, s_ref, o_ref):
    y = _hdot(c_ref[...], br_ref[...]) - _hdot(s_ref[...], bi_ref[...])
    o_ref[...] = x0_ref[...] * (y + m_ref[...] * d_ref[...])


def idft_stage1(br, bi, m, x0, d_cols, L):
    n, n1, n2 = _dft_plan(L)
    cs = _dft_consts(L)
    bsz, _, cols = br.shape
    tc = min(cols, 4096)
    b_spec = pl.BlockSpec((None, n1, tc), lambda b, j: (b, 0, j))
    t_spec = pl.BlockSpec((None, n1 // 2, tc), lambda b, j: (b, 0, j))
    c_spec = pl.BlockSpec((n1 // 2, n1), lambda b, j: (0, 0))
    return pl.pallas_call(
        _idft1_kernel,
        grid=(bsz, cols // tc),
        in_specs=[b_spec, b_spec, t_spec, t_spec, pl.BlockSpec((1, tc), lambda b, j: (0, j)), c_spec, c_spec],
        out_specs=t_spec,
        out_shape=jax.ShapeDtypeStruct((bsz, n1 // 2, cols), F32),
        compiler_params=_cparams(("parallel", "parallel"), 32),
        name="idft_stage1",
    )(br, bi, m, x0, d_cols, cs["c1t"], cs["s1t"])


def hyena_mix(p, lp):
    bsz, L, _ = p.shape
    n, n1, n2 = _dft_plan(L)
    w = HY_WIDTH
    taps, nrm = hy_filter_taps(L, lp)
    far, fai = dft_stage1(taps.reshape(2, n1 // 2, n2 * w), L)
    kr, ki = dft_stage2_filter(far.reshape(2, n, w), fai.reshape(2, n, w), 1.0 / nrm, L)
    m, x0 = hy_pre(p, lp["hy_conv_w"], lp["hy_conv_b"])
    mv = m.reshape(bsz, n1 // 2, n2 * w)
    ar, ai = dft_stage1(mv, L)
    br, bi = dft_stage2_conv(ar.reshape(bsz, n, w), ai.reshape(bsz, n, w), kr, ki, L)
    d_cols = jnp.tile(lp["hy_d"].astype(F32).reshape(1, w), (1, n2))
    out = idft_stage1(br.reshape(bsz, n1, n2 * w), bi.reshape(bsz, n1, n2 * w), mv,
                      x0.reshape(bsz, n1 // 2, n2 * w), d_cols, L)
    return out.reshape(bsz, L, w)


def _gelu_tanh(x):
    return 0.5 * x * (1.0 + jnp.tanh(math.sqrt(2.0 / math.pi) * (x + 0.044715 * (x * x * x))))


def _merge_kernel(x_ref, y5_ref, hy_ref,
                  sx_ref, sxp_ref, sxn_ref, sb_ref, sc_ref, scp_ref, scn_ref,
                  g0a_ref, g0b_ref, g1a_ref, g1b_ref, g2a_ref, g2b_ref, mg_ref,
                  glu_ref, scw_ref, sco_ref, hyo_ref, ow_ref, o_ref):
    i = pl.program_id(1)
    first = i == 0
    last = i == pl.num_programs(1) - 1
    ag = _bdot(_gelu_tanh(y5_ref[...]), glu_ref[...])
    y_s5 = ag[:, :D_MODEL] * jax.nn.sigmoid(ag[:, D_MODEL:])
    l7 = slice(SUBLANES - 1, SUBLANES)
    cx = sc_ref[...] * sx_ref[...]
    cxp = scp_ref[l7, :] * sxp_ref[l7, :]
    cxn = scn_ref[0:1, :] * sxn_ref[0:1, :]
    y_sc = _bdot(sb_ref[...] * _conv3(cx, cxp, cxn, scw_ref[...], first, last), sco_ref[...])
    y_hy = _bdot(hy_ref[...], hyo_ref[...])
    gate = lambda a, b: jax.nn.sigmoid(jnp.concatenate([a[...], b[...]], axis=1))
    m = gate(g0a_ref, g0b_ref) * y_s5 + gate(g1a_ref, g1b_ref) * y_sc + gate(g2a_ref, g2b_ref) * y_hy
    o_ref[...] = x_ref[...] + mg_ref[...] * _bdot(m, ow_ref[...])


def merge(x, p, y5, hy, mod3, wl, ctx):
    bsz, L, _ = x.shape
    tl = min(L, 256)
    row = lambda width, col: pl.BlockSpec((None, tl, width), lambda b, i: (b, i, col))
    full = lambda shape: pl.BlockSpec(shape, lambda b, i: (0,) * len(shape))
    c_sc = OFF_SC // SC_WIDTH
    half = D_MODEL // 2
    c_g = OFF_GATE // half
    specs = [row(D_MODEL, 0), row(S5_WIDTH, 0), row(HY_WIDTH, 0)]
    specs += _halo_specs(tl, L, SC_WIDTH, c_sc)
    specs += [row(SC_WIDTH, c_sc + 1)]
    specs += _halo_specs(tl, L, SC_WIDTH, c_sc + 2)
    specs += [row(half, c_g + k) for k in range(2 * N_BRANCH)] + [_mod_spec(2, ctx)]
    specs += [full((S5_WIDTH, 2 * D_MODEL)), full((3, SC_WIDTH)), full((SC_WIDTH, D_MODEL)),
              full((HY_WIDTH, D_MODEL)), full((D_MODEL, D_MODEL))]
    return pl.pallas_call(
        _merge_kernel,
        grid=(bsz, L // tl),
        in_specs=specs,
        out_specs=row(D_MODEL, 0),
        out_shape=jax.ShapeDtypeStruct((bsz, L, D_MODEL), F32),
        compiler_params=_cparams(("parallel", "parallel"), 48),
        name="merge",
    )(x, y5, hy, *([p] * 13), mod3,
      wl["s5_glu_w"], wl["sc_conv_w"], wl["sc_out_w"], wl["hy_out_w"], wl["out_w"])


def _mlp_kernel(x_ref, g_ref, sh_ref, sc_ref, mg_ref, w1_ref, w2_ref, fg_ref, o_ref, h_ref, acc_ref,
                *, final_norm):
    j = pl.program_id(2)

    @pl.when(j == 0)
    def _():
        h_ref[...] = _norm_mod(x_ref[...], g_ref[...], sh_ref[...], sc_ref[...]).astype(BF16)
        acc_ref[...] = jnp.zeros_like(acc_ref)

    r = jnp.maximum(jnp.dot(h_ref[...], w1_ref[...], preferred_element_type=F32), 0.0)
    acc_ref[...] += _bdot(r * r, w2_ref[...])

    @pl.when(j == pl.num_programs(2) - 1)
    def _():
        y = x_ref[...] + mg_ref[...] * acc_ref[...]
        if final_norm:
            y = (y * lax.rsqrt(jnp.mean(y * y, axis=-1, keepdims=True) + EPS)) * fg_ref[...]
        o_ref[...] = y


def mlp(x, g, mod3, w1, w2, final_g, ctx, final_norm):
    bsz, L, _ = x.shape
    tl = min(L, 1024)
    tf = 1024
    x_spec = pl.BlockSpec((None, tl, D_MODEL), lambda b, i, j: (b, i, 0))
    vec = pl.BlockSpec((1, D_MODEL), lambda b, i, j: (0, 0))
    return pl.pallas_call(
        functools.partial(_mlp_kernel, final_norm=final_norm),
        grid=(bsz, L // tl, D_FF // tf),
        in_specs=[x_spec, vec, _mod_spec(3, ctx), _mod_spec(4, ctx), _mod_spec(5, ctx),
                  pl.BlockSpec((D_MODEL, tf), lambda b, i, j: (0, j)),
                  pl.BlockSpec((tf, D_MODEL), lambda b, i, j: (j, 0)),
                  vec],
        out_specs=x_spec,
        out_shape=jax.ShapeDtypeStruct((bsz, L, D_MODEL), F32),
        scratch_shapes=[pltpu.VMEM((tl, D_MODEL), BF16), pltpu.VMEM((tl, D_MODEL), F32)],
        compiler_params=_cparams(("parallel", "parallel", "arbitrary"), 48),
        name="mlp",
    )(x, g.reshape(1, D_MODEL), mod3, mod3, mod3, w1, w2, final_g.reshape(1, D_MODEL))


def _sincos_2d(rows, cols, dim):
    quarter = dim // 4
    omega = 1.0 / (10000.0 ** (jnp.arange(quarter, dtype=F32) / quarter))
    er = jnp.arange(rows, dtype=F32)[:, None] * omega[None]
    ec = jnp.arange(cols, dtype=F32)[:, None] * omega[None]
    er = jnp.concatenate([jnp.sin(er), jnp.cos(er)], axis=-1)
    ec = jnp.concatenate([jnp.sin(ec), jnp.cos(ec)], axis=-1)
    emb = jnp.concatenate([
        jnp.broadcast_to(er[:, None, :], (rows, cols, dim // 2)),
        jnp.broadcast_to(ec[None, :, :], (rows, cols, dim // 2))], axis=-1)
    return emb.reshape(rows * cols, dim)


def kernel(x, c, ctx, c_ctx, ada_w, ada_b, norm1_g, norm2_g, w_in, s5_a_re, s5_a_im, s5_log_dt, s5_b_re, s5_b_im, s5_c_re, s5_c_im, s5_d, s5_glu_w, sc_conv_w, sc_out_w, hy_conv_w, hy_conv_b, hy_f_w1, hy_f_b1, hy_f_w2, hy_f_b2, hy_f_w3, hy_f_b3, hy_f_freq, hy_f_w4, hy_d, hy_out_w, out_w, mlp_w1, mlp_w2, final_g):
    bsz, L, _ = x.shape
    x = x + _sincos_2d(L // GRID_W, GRID_W, D_MODEL)[None]
    xc = ctx
    cc = jnp.concatenate([c, c_ctx[None, :], jnp.zeros((MOD_ROWS - bsz - 1, D_MODEL), F32)], axis=0)
    mod_all = ada_mod(cc, ada_w.astype(BF16), ada_b)
    for l in range(DEPTH):
        last = l == DEPTH - 1
        mod3 = mod_all[l].reshape(MOD_ROWS, 1, N_MOD * D_MODEL)
        w_in_l = w_in[l].astype(BF16)
        wl = {"s5_glu_w": s5_glu_w[l].astype(BF16), "sc_conv_w": sc_conv_w[l],
              "sc_out_w": sc_out_w[l].astype(BF16), "hy_out_w": hy_out_w[l].astype(BF16),
              "out_w": out_w[l].astype(BF16)}
        lp = {"hy_conv_w": hy_conv_w[l], "hy_conv_b": hy_conv_b[l],
              "hy_f_w1": hy_f_w1[l], "hy_f_b1": hy_f_b1[l], "hy_f_w2": hy_f_w2[l], "hy_f_b2": hy_f_b2[l],
              "hy_f_w3": hy_f_w3[l], "hy_f_b3": hy_f_b3[l], "hy_f_freq": hy_f_freq[l],
              "hy_f_w4": hy_f_w4[l], "hy_d": hy_d[l]}
        w1 = mlp_w1[l].astype(BF16)
        w2 = mlp_w2[l].astype(BF16)
        ops = _s5_operators(s5_a_re[l], s5_a_im[l], s5_log_dt[l], s5_b_re[l], s5_b_im[l],
                            s5_c_re[l], s5_c_im[l], s5_d[l])

        pc = norm_mm(xc, norm1_g[l], mod3, w_in_l[:, :S5_WIDTH] if last else w_in_l, ctx=True)
        yc5, ctx_final = s5_layer(pc[..., OFF_S5:OFF_S5 + S5_WIDTH], ops, None, not last)

        p = norm_mm(x, norm1_g[l], mod3, w_in_l, ctx=False)
        y5, _ = s5_layer(p[..., OFF_S5:OFF_S5 + S5_WIDTH], ops, ctx_final, True)
        x = merge(x, p, y5, hyena_mix(p, lp), mod3, wl, ctx=False)
        x = mlp(x, norm2_g[l], mod3, w1, w2, final_g, ctx=False, final_norm=last)

        if not last:
            xc = merge(xc, pc, yc5, hyena_mix(pc, lp), mod3, wl, ctx=True)
            xc = mlp(xc, norm2_g[l], mod3, w1, w2, final_g, ctx=True, final_norm=False)
    return x
```

```python
import functools
import math

import numpy as np
import jax
import jax.numpy as jnp
from jax import lax
from jax.experimental import pallas as pl
from jax.experimental.pallas import tpu as pltpu

F32 = jnp.float32
BF16 = jnp.bfloat16
HIGHEST = lax.Precision.HIGHEST

D_MODEL = 1024
DEPTH = 2
GRID_W = 64
S5_WIDTH = 512
S5_GROUP = 16
S5_GROUPS = S5_WIDTH // S5_GROUP
S5_STATE = 64
SC_WIDTH = 512
HY_WIDTH = 512
HY_BANDS = 16
HY_EMB = 1 + 2 * HY_BANDS
HY_FAST_DECAY = 0.3
HY_SLOW_DECAY = 1.5
HY_DECAY_TARGET = 1e-2
HY_DECAY_SHIFT = 0.05
HY_MAX_DECAY = math.log(HY_DECAY_TARGET) / HY_FAST_DECAY
HY_MIN_DECAY = math.log(HY_DECAY_TARGET) / HY_SLOW_DECAY
N_BRANCH = 3
D_FF = 4 * D_MODEL
N_MOD = 6
EPS = 1e-6
OFF_S5 = 0
OFF_SC = OFF_S5 + S5_WIDTH
OFF_HY = OFF_SC + 3 * SC_WIDTH
OFF_GATE = OFF_HY + 3 * HY_WIDTH
D_IN = OFF_GATE + N_BRANCH * D_MODEL

LANES = 128
SUBLANES = 8
S5_T = 8
S5_BLOCKS = S5_WIDTH // LANES
S5_HALF = 256
EMB_PAD = 40
MOD_ROWS = 8
CTX_ROW = 4
HY_N2 = 256
ROW_TILE = 256


def _cparams(sem, vmem_mb):
    return pltpu.CompilerParams(dimension_semantics=sem, vmem_limit_bytes=vmem_mb * 1024 * 1024)


def _bdot(a, b):
    return jnp.dot(a.astype(BF16), b.astype(BF16), preferred_element_type=F32)


def _hdot(a, b):
    return jnp.dot(a, b, precision=HIGHEST, preferred_element_type=F32)


def _ada_kernel(c_ref, w_ref, b_ref, o_ref):
    c = c_ref[...]
    s = c * jax.nn.sigmoid(c)
    o_ref[...] = _bdot(s, w_ref[...]) + b_ref[...]


def ada_mod(cc, ada_w, ada_b):
    n = N_MOD * D_MODEL
    tn = 1536
    return pl.pallas_call(
        _ada_kernel,
        grid=(DEPTH, n // tn),
        in_specs=[
            pl.BlockSpec((MOD_ROWS, D_MODEL), lambda l, j: (0, 0)),
            pl.BlockSpec((None, D_MODEL, tn), lambda l, j: (l, 0, j)),
            pl.BlockSpec((None, 1, tn), lambda l, j: (l, 0, j)),
        ],
        out_specs=pl.BlockSpec((None, MOD_ROWS, tn), lambda l, j: (l, 0, j)),
        out_shape=jax.ShapeDtypeStruct((DEPTH, MOD_ROWS, n), F32),
        compiler_params=_cparams(("parallel", "parallel"), 40),
        name="ada_mod",
    )(cc, ada_w, ada_b.reshape(DEPTH, 1, n))


def _mod_spec(chunk, ctx):
    if ctx:
        return pl.BlockSpec((None, 1, D_MODEL), lambda b, *_: (CTX_ROW, 0, chunk))
    return pl.BlockSpec((None, 1, D_MODEL), lambda b, *_: (b, 0, chunk))


def _norm_mod(x, g, sh, sc):
    y = x * lax.rsqrt(jnp.mean(x * x, axis=-1, keepdims=True) + EPS)
    return (y * g) * (1.0 + sc) + sh


def _norm_mm_kernel(x_ref, g_ref, sh_ref, sc_ref, w_ref, o_ref, h_ref):
    @pl.when(pl.program_id(2) == 0)
    def _():
        h_ref[...] = _norm_mod(x_ref[...], g_ref[...], sh_ref[...], sc_ref[...]).astype(BF16)

    o_ref[...] = jnp.dot(h_ref[...], w_ref[...], preferred_element_type=F32)


def norm_mm(x, g, mod3, w, ctx):
    bsz, L, _ = x.shape
    n = w.shape[1]
    tl = min(L, 1024)
    tn = n // 4 if n % (4 * LANES) == 0 and n > 2048 else n
    return pl.pallas_call(
        _norm_mm_kernel,
        grid=(bsz, L // tl, n // tn),
        in_specs=[
            pl.BlockSpec((None, tl, D_MODEL), lambda b, i, j: (b, i, 0)),
            pl.BlockSpec((1, D_MODEL), lambda b, i, j: (0, 0)),
            _mod_spec(0, ctx),
            _mod_spec(1, ctx),
            pl.BlockSpec((D_MODEL, tn), lambda b, i, j: (0, j)),
        ],
        out_specs=pl.BlockSpec((None, tl, tn), lambda b, i, j: (b, i, j)),
        out_shape=jax.ShapeDtypeStruct((bsz, L, n), F32),
        scratch_shapes=[pltpu.VMEM((tl, D_MODEL), BF16)],
        compiler_params=_cparams(("parallel", "parallel", "arbitrary"), 48),
        name="norm_mm",
    )(x, g.reshape(1, D_MODEL), mod3, mod3, w)


def _s5_operators(a_re, a_im, log_dt, b_re, b_im, c_re, c_im, d_skip):
    T = S5_T
    G, N, P = S5_GROUPS, S5_STATE, S5_GROUP
    taus = jnp.arange(T + 1, dtype=F32)[:, None, None]
    ops = []
    for k in range(2):
        ar, ai = a_re[k].astype(F32), a_im[k].astype(F32)
        dt = jnp.exp(log_dt[k].astype(F32))[:, None]
        mag = jnp.exp(ar * dt * taus)
        ang = ai * dt * taus
        pr, pi = mag * jnp.cos(ang), mag * jnp.sin(ang)
        den = ar * ar + ai * ai
        nr, ni = pr[1] - 1.0, pi[1]
        f_re = (nr * ar + ni * ai) / den
        f_im = (ni * ar - nr * ai) / den
        br, bi = b_re[k].astype(F32), b_im[k].astype(F32)
        bbr = f_re[..., None] * br - f_im[..., None] * bi
        bbi = f_re[..., None] * bi + f_im[..., None] * br
        er = pr[..., None] * bbr - pi[..., None] * bbi
        ei = pr[..., None] * bbi + pi[..., None] * bbr
        cr, ci = c_re[k].astype(F32), c_im[k].astype(F32)
        kk = (jnp.einsum("gqn,tgnp->tgqp", cr, er[:T], precision=HIGHEST)
              - jnp.einsum("gqn,tgnp->tgqp", ci, ei[:T], precision=HIGHEST))
        car = cr[None] * pr[:, :, None, :] - ci[None] * pi[:, :, None, :]
        cai = cr[None] * pi[:, :, None, :] + ci[None] * pr[:, :, None, :]
        ops.append((pr, pi, er, ei, kk, car, cai))

    eye8 = jnp.eye(8, dtype=F32)

    def block_diag_k(kk):
        k5 = kk.reshape(T, S5_BLOCKS, 8, P, P)
        out = jnp.einsum("tbgqp,gh->tbgphq", k5, eye8)
        return out.reshape(T, S5_BLOCKS, LANES, LANES)

    kbf, kbr = block_diag_k(ops[0][4]), block_diag_k(ops[1][4])
    j_in = jnp.arange(T)[:, None]
    j_out = jnp.arange(T)[None, :]
    tau = j_out - j_in
    mf = jnp.where((tau >= 0)[..., None, None, None], kbf[jnp.clip(tau, 0, T - 1)], 0.0)
    mr = jnp.where((tau <= 0)[..., None, None, None], kbr[jnp.clip(-tau, 0, T - 1)], 0.0)
    dsk = d_skip.astype(F32).reshape(S5_BLOCKS, LANES)
    dmat = jnp.einsum("bc,cd->bcd", dsk, jnp.eye(LANES, dtype=F32))
    md = jnp.where((tau == 0)[..., None, None, None], dmat[None, None], 0.0)
    m_all = mf + mr + md
    m_op = m_all.transpose(2, 0, 3, 1, 4).reshape(S5_BLOCKS, T * LANES, T * LANES)

    oh = (jnp.arange(8)[None, None, :] == (jnp.arange(2)[:, None, None] * 4 + jnp.arange(4)[None, :, None])).astype(F32)

    def b_mat(er, ei, sel):
        def one(e):
            e6 = e[sel].reshape(T, S5_BLOCKS, 2, 4, N, P)
            return jnp.einsum("jbhlnp,hlg->bhjgpln", e6, oh)
        both = jnp.stack([one(er), one(ei)], axis=5)
        return both.reshape(S5_BLOCKS, 2, T * LANES, 2 * S5_HALF)

    def c_mat(car, cai, sel):
        def one(c):
            c6 = c[sel].reshape(T, S5_BLOCKS, 2, 4, P, N)
            return jnp.einsum("jbhlqn,hlg->bhlnjgq", c6, oh)
        both = jnp.stack([one(car), -one(cai)], axis=2)
        return both.reshape(S5_BLOCKS, 4 * S5_HALF, T * LANES)

    jj = jnp.arange(T)
    b_f = b_mat(ops[0][2], ops[0][3], T - 1 - jj)
    b_r = b_mat(ops[1][2], ops[1][3], jj)
    c_f = c_mat(ops[0][5], ops[0][6], jj + 1)
    c_r = c_mat(ops[1][5], ops[1][6], T - jj)

    def at(pr, pi):
        r = pr[T].reshape(S5_BLOCKS, 2, 1, S5_HALF)
        i = pi[T].reshape(S5_BLOCKS, 2, 1, S5_HALF)
        return jnp.concatenate([r, i], axis=-1)

    return dict(m=m_op.astype(BF16), b_f=b_f.astype(BF16), b_r=b_r.astype(BF16),
                c_f=c_f.astype(BF16), c_r=c_r.astype(BF16),
                at_f=at(ops[0][0], ops[0][1]), at_r=at(ops[1][0], ops[1][1]))


def _cmul_add(a, s, d):
    h = S5_HALF
    ar, ai = a[:, :h], a[:, h:]
    sr, si = s[:, :h], s[:, h:]
    return jnp.concatenate([ar * sr - ai * si, ar * si + ai * sr], axis=1) + d


def _s5_scan_kernel(x_ref, bf_ref, br_ref, atf_ref, atr_ref, if_ref, ir_ref,
                    s_ref, r_ref, ff_ref, fr_ref, *, rows):
    rc = min(rows, 256)
    for r0 in range(0, rows, rc):
        xs = x_ref[r0:r0 + rc, :]
        s_ref[r0:r0 + rc, :] = jnp.dot(xs, bf_ref[...], preferred_element_type=F32)
        r_ref[r0:r0 + rc, :] = jnp.dot(xs, br_ref[...], preferred_element_type=F32)

    ntile = rows // SUBLANES
    atf = atf_ref[...]
    atr = atr_ref[...]
    low = lax.broadcasted_iota(jnp.int32, (SUBLANES, 2 * S5_HALF), 0) < 4

    def body(i, carry):
        s, r = carry
        fo = pl.multiple_of(i * SUBLANES, SUBLANES)
        d = s_ref[pl.ds(fo, SUBLANES), :]
        t1 = pltpu.roll(_cmul_add(atf, s, d), 4, 0)
        t2 = _cmul_add(atf, t1, d)
        s_ref[pl.ds(fo, SUBLANES), :] = jnp.where(low, s, t1)
        ro = pl.multiple_of((ntile - 1 - i) * SUBLANES, SUBLANES)
        e = r_ref[pl.ds(ro, SUBLANES), :]
        u1 = pltpu.roll(_cmul_add(atr, r, e), 4, 0)
        u2 = _cmul_add(atr, u1, e)
        r_ref[pl.ds(ro, SUBLANES), :] = jnp.where(low, u1, r)
        return pltpu.roll(t2, 4, 0), pltpu.roll(u2, 4, 0)

    s0 = if_ref[...]
    r0 = pltpu.roll(ir_ref[...], 4, 0)
    s, r = lax.fori_loop(0, ntile, body, (s0, r0))
    ff_ref[...] = s
    fr_ref[...] = pltpu.roll(r, 4, 0)


def s5_scan(x4, ops, init_f, init_r):
    nb, rows, tk = x4.shape
    w = 2 * S5_HALF
    kern = functools.partial(_s5_scan_kernel, rows=rows)
    st_spec = pl.BlockSpec((None, None, SUBLANES, w), lambda b, h: (b, h, 0, 0))
    out_shape = (
        jax.ShapeDtypeStruct((nb, rows, 2 * w), F32),
        jax.ShapeDtypeStruct((nb, rows, 2 * w), F32),
        jax.ShapeDtypeStruct((nb, 2, SUBLANES, w), F32),
        jax.ShapeDtypeStruct((nb, 2, SUBLANES, w), F32),
    )
    return pl.pallas_call(
        kern,
        grid=(nb, 2),
        in_specs=[
            pl.BlockSpec((None, rows, tk), lambda b, h: (b, 0, 0)),
            pl.BlockSpec((None, None, tk, w), lambda b, h: (b, h, 0, 0)),
            pl.BlockSpec((None, None, tk, w), lambda b, h: (b, h, 0, 0)),
            pl.BlockSpec((None, None, 1, w), lambda b, h: (b, h, 0, 0)),
            pl.BlockSpec((None, None, 1, w), lambda b, h: (b, h, 0, 0)),
            st_spec, st_spec,
        ],
        out_specs=(
            pl.BlockSpec((None, rows, w), lambda b, h: (b, 0, h)),
            pl.BlockSpec((None, rows, w), lambda b, h: (b, 0, h)),
            st_spec, st_spec,
        ),
        out_shape=out_shape,
        compiler_params=_cparams(("parallel", "parallel"), 48),
        name="s5_scan",
    )(x4, ops["b_f"], ops["b_r"], ops["at_f"], ops["at_r"], init_f, init_r)


def _s5_out_kernel(x_ref, s_ref, r_ref, m_ref, cf_ref, cr_ref, o_ref):
    acc = jnp.dot(x_ref[...], m_ref[...], preferred_element_type=F32)
    acc += jnp.dot(s_ref[...].astype(BF16), cf_ref[...], preferred_element_type=F32)
    acc += jnp.dot(r_ref[...].astype(BF16), cr_ref[...], preferred_element_type=F32)
    o_ref[...] = acc


def s5_out(x4, s_st, r_st, ops):
    nb, rows, tk = x4.shape
    tr = min(rows, 512)
    w = 4 * S5_HALF
    return pl.pallas_call(
        _s5_out_kernel,
        grid=(nb, rows // tr),
        in_specs=[
            pl.BlockSpec((None, tr, tk), lambda b, i: (b, i, 0)),
            pl.BlockSpec((None, tr, w), lambda b, i: (b, i, 0)),
            pl.BlockSpec((None, tr, w), lambda b, i: (b, i, 0)),
            pl.BlockSpec((None, tk, tk), lambda b, i: (b, 0, 0)),
            pl.BlockSpec((None, w, tk), lambda b, i: (b, 0, 0)),
            pl.BlockSpec((None, w, tk), lambda b, i: (b, 0, 0)),
        ],
        out_specs=pl.BlockSpec((None, tr, tk), lambda b, i: (b, i, 0)),
        out_shape=jax.ShapeDtypeStruct((nb, rows, tk), F32),
        compiler_params=_cparams(("parallel", "parallel"), 48),
        name="s5_out",
    )(x4, s_st, r_st, ops["m"], ops["c_f"], ops["c_r"])


def _to_chunk_rows(u):
    bsz, L, _ = u.shape
    nc = L // S5_T
    x = u.astype(BF16).reshape(bsz, nc, S5_T, S5_BLOCKS, LANES)
    return x.transpose(3, 1, 0, 2, 4).reshape(S5_BLOCKS, nc * bsz, S5_T * LANES)


def _from_chunk_rows(y4, bsz):
    nb, rows, _ = y4.shape
    nc = rows // bsz
    y = y4.reshape(nb, nc, bsz, S5_T, LANES).transpose(2, 1, 3, 0, 4)
    return y.reshape(bsz, nc * S5_T, S5_WIDTH)


def s5_layer(u, ops, init, readout):
    bsz = u.shape[0]
    x4 = _to_chunk_rows(u)
    if init is None:
        z = jnp.zeros((S5_BLOCKS, 2, SUBLANES, 2 * S5_HALF), F32)
        init = (z, z)
    s_st, r_st, fin_f, fin_r = s5_scan(x4, ops, init[0], init[1])
    y = _from_chunk_rows(s5_out(x4, s_st, r_st, ops), bsz) if readout else None
    return y, (fin_f, fin_r)


def _conv3(x, prev_row, next_row, w, first, last):
    n = x.shape[0]
    rows = lax.broadcasted_iota(jnp.int32, x.shape, 0)
    pz = jnp.where(first, 0.0, prev_row)
    nz = jnp.where(last, 0.0, next_row)
    xp = jnp.where(rows == 0, pz, pltpu.roll(x, 1, 0))
    xn = jnp.where(rows == n - 1, nz, pltpu.roll(x, n - 1, 0))
    return w[0:1, :] * xp + w[1:2, :] * x + w[2:3, :] * xn


def _halo_specs(tl, L, width, col):
    tb = tl // SUBLANES
    nb = L // SUBLANES
    cur = pl.BlockSpec((None, tl, width), lambda b, i: (b, i, col))
    prv = pl.BlockSpec((None, SUBLANES, width), lambda b, i: (b, jnp.maximum(i * tb - 1, 0), col))
    nxt = pl.BlockSpec((None, SUBLANES, width), lambda b, i: (b, jnp.minimum((i + 1) * tb, nb - 1), col))
    return [cur, prv, nxt]


def _dft_plan(L):
    n = 2 * L
    n1 = n // HY_N2
    return n, n1, n1 // 2, n1 // 2 + 1


def _split_bf16(a):
    hi = np.asarray(a, np.float32).astype(BF16)
    lo = (np.asarray(a, np.float32) - hi.astype(np.float32)).astype(BF16)
    return hi, lo


@functools.lru_cache(maxsize=None)
def _dft_consts(L):
    n, n1, h, nk = _dft_plan(L)
    k1 = np.arange(nk)[:, None]
    m1 = np.arange(h)[None, :]
    th = 2.0 * np.pi * k1 * m1 / n1
    wt = np.where((k1 == 0) | (k1 == h), 1.0, 2.0) / n
    k2 = np.arange(HY_N2)[:, None]
    m2 = np.arange(HY_N2)[None, :]
    ph = -2.0 * np.pi * (k2 * m2 / HY_N2)[None] - 2.0 * np.pi * np.arange(nk)[:, None, None] * m2[None] / n
    gr, gi = np.cos(ph), np.sin(ph)
    wf = np.concatenate([gr.transpose(0, 2, 1), gi.transpose(0, 2, 1)], axis=2)
    wi = np.concatenate([gr, -gi], axis=2)
    f = lambda a: np.asarray(a, np.float32)
    wf_hi, wf_lo = _split_bf16(wf)
    wi_hi, wi_lo = _split_bf16(wi)
    return dict(c1=f(np.cos(th)), s1=f(-np.sin(th)),
                ic=f((wt * np.cos(th)).T), isn=f((wt * np.sin(th)).T),
                wf_hi=wf_hi, wf_lo=wf_lo, wi_hi=wi_hi, wi_lo=wi_lo)


HY_RC = 32
W = HY_WIDTH


def _row_chunks(body):
    def step(ci, carry):
        body(pl.multiple_of(ci * HY_RC, HY_RC))
        return carry
    lax.fori_loop(0, W // HY_RC, step, 0)


def _store_split(hi_ref, lo_ref, r0, vr, vi):
    for off, v in ((0, vr), (W, vi)):
        hi = v.astype(BF16)
        hi_ref[pl.ds(off + r0, HY_RC), :] = hi
        lo_ref[pl.ds(off + r0, HY_RC), :] = (v - hi.astype(F32)).astype(BF16)


def _stage1(x_ref, c_ref, s_ref, k, h, hi_ref, lo_ref):
    def body(r0):
        x0 = x_ref[0, pl.ds(r0, HY_RC), :]
        ar = c_ref[k, 0] * x0
        ai = s_ref[k, 0] * x0
        for m in range(1, h):
            xm = x_ref[m, pl.ds(r0, HY_RC), :]
            ar += c_ref[k, m] * xm
            ai += s_ref[k, m] * xm
        _store_split(hi_ref, lo_ref, r0, ar, ai)
    _row_chunks(body)


def _cmatmul(hi_ref, lo_ref, w_hi, w_lo, r_ref):
    a_hi = hi_ref[...]
    r_ref[...] = (jnp.dot(a_hi, w_hi, preferred_element_type=F32)
                  + jnp.dot(lo_ref[...], w_hi, preferred_element_type=F32)
                  + jnp.dot(a_hi, w_lo, preferred_element_type=F32))


def _cresult(r_ref, r0):
    top = r_ref[pl.ds(r0, HY_RC), :]
    bot = r_ref[pl.ds(W + r0, HY_RC), :]
    return top[:, :HY_N2] - bot[:, HY_N2:], top[:, HY_N2:] + bot[:, :HY_N2]


def _hy_pre_kernel(v_ref, vp_ref, vn_ref, a_ref, ap_ref, an_ref, z_ref, zp_ref, zn_ref,
                   w_ref, b_ref, mt_ref, x0_ref):
    i = pl.program_id(1)
    first = i == 0
    last = i == pl.num_programs(1) - 1
    w = w_ref[...]
    b = b_ref[...]

    def part(c, p, n, k):
        sl = slice(k * HY_WIDTH, (k + 1) * HY_WIDTH)
        return _conv3(c[...], p[SUBLANES - 1:SUBLANES, :], n[0:1, :], w[:, sl], first, last) + b[:, sl]

    v = part(v_ref, vp_ref, vn_ref, 0)
    x1 = part(a_ref, ap_ref, an_ref, 1)
    mt_ref[...] = (x1 * v).T
    x0_ref[...] = part(z_ref, zp_ref, zn_ref, 2)


def hy_pre(p, conv_w, conv_b):
    bsz, L, _ = p.shape
    tl = ROW_TILE
    c0 = OFF_HY // HY_WIDTH
    specs = []
    for k in range(3):
        specs += _halo_specs(tl, L, HY_WIDTH, c0 + k)
    specs += [pl.BlockSpec((3, 3 * HY_WIDTH), lambda b, i: (0, 0)),
              pl.BlockSpec((1, 3 * HY_WIDTH), lambda b, i: (0, 0))]
    return pl.pallas_call(
        _hy_pre_kernel,
        grid=(bsz, L // tl),
        in_specs=specs,
        out_specs=(pl.BlockSpec((None, None, HY_WIDTH, HY_N2), lambda b, i: (b, i, 0, 0)),
                   pl.BlockSpec((None, tl, HY_WIDTH), lambda b, i: (b, i, 0))),
        out_shape=(jax.ShapeDtypeStruct((bsz, L // HY_N2, HY_WIDTH, HY_N2), F32),
                   jax.ShapeDtypeStruct((bsz, L, HY_WIDTH), F32)),
        compiler_params=_cparams(("parallel", "parallel"), 32),
        name="hy_pre",
    )(*([p] * 9), conv_w, conv_b.reshape(1, -1))


def _hy_filter_kernel(z_ref, w1_ref, b1_ref, w2_ref, b2_ref, w3_ref, b3_ref, fr_ref, w4_ref,
                      dl_ref, hf_ref, hb_ref, nrm_ref, *, tl):
    i = pl.program_id(0)
    z = z_ref[...]
    fr = fr_ref[...]
    h = jnp.sin(fr * (_hdot(z, w1_ref[...]) + b1_ref[...]))
    h = jnp.sin(fr * (_hdot(h, w2_ref[...]) + b2_ref[...]))
    h = jnp.sin(fr * (_hdot(h, w3_ref[...]) + b3_ref[...]))
    h = _hdot(h, w4_ref[...])
    t = z[:, 0:1]
    window = jnp.exp(-t * dl_ref[...]) + HY_DECAY_SHIFT
    hf = h[:, :HY_WIDTH] * window
    rows = lax.broadcasted_iota(jnp.int32, (tl, HY_WIDTH), 0) + i * tl
    hb = jnp.where(rows == 0, 0.0, h[:, HY_WIDTH:] * window)
    hf_ref[...] = hf.T
    hb_ref[...] = hb.T
    part = jnp.sum(jnp.abs(hf) + jnp.abs(hb), axis=0, keepdims=True)

    @pl.when(i == 0)
    def _():
        nrm_ref[...] = jnp.zeros_like(nrm_ref)

    nrm_ref[...] += part


def hy_filter_taps(L, lp):
    t = jnp.linspace(0.0, 1.0, L, dtype=F32)[:, None]
    ang = 2.0 * math.pi * jnp.arange(L, dtype=F32)[:, None] / L
    bands = jnp.linspace(1e-4, HY_BANDS - 1, HY_BANDS, dtype=F32)[None, :]
    z = jnp.concatenate([t, jnp.cos(bands * ang), -jnp.sin(bands * ang),
                         jnp.zeros((L, EMB_PAD - HY_EMB), F32)], axis=-1)
    w1 = jnp.concatenate([lp["hy_f_w1"], jnp.zeros((EMB_PAD - HY_EMB, lp["hy_f_w1"].shape[1]), F32)], axis=0)
    deltas = jnp.abs(jnp.linspace(HY_MIN_DECAY, HY_MAX_DECAY, HY_WIDTH, dtype=F32))[None, :]
    tl = HY_N2
    hid = w1.shape[1]
    full = lambda shape: pl.BlockSpec(shape, lambda i: (0,) * len(shape))
    o_spec = pl.BlockSpec((None, HY_WIDTH, HY_N2), lambda i: (i, 0, 0))
    shp = jax.ShapeDtypeStruct((L // HY_N2, HY_WIDTH, HY_N2), F32)
    return pl.pallas_call(
        functools.partial(_hy_filter_kernel, tl=tl),
        grid=(L // tl,),
        in_specs=[pl.BlockSpec((tl, EMB_PAD), lambda i: (i, 0)),
                  full((EMB_PAD, hid)), full((1, hid)), full((hid, hid)), full((1, hid)),
                  full((hid, hid)), full((1, hid)), full((1, hid)), full((hid, 2 * HY_WIDTH)),
                  full((1, HY_WIDTH))],
        out_specs=(o_spec, o_spec, pl.BlockSpec((1, HY_WIDTH), lambda i: (0, 0))),
        out_shape=(shp, shp, jax.ShapeDtypeStruct((1, HY_WIDTH), F32)),
        compiler_params=_cparams(("arbitrary",), 32),
        name="hy_filter_taps",
    )(z, w1, lp["hy_f_b1"].reshape(1, -1), lp["hy_f_w2"], lp["hy_f_b2"].reshape(1, -1),
      lp["hy_f_w3"], lp["hy_f_b3"].reshape(1, -1), lp["hy_f_freq"].reshape(1, -1), lp["hy_f_w4"], deltas)


def _hy_spectrum_kernel(c_ref, s_ref, hf_ref, hb_ref, whi_ref, wlo_ref, inv_ref, k_ref,
                        hi_ref, lo_ref, rf_ref, rb_ref, *, h):
    k = pl.program_id(0)
    _stage1(hf_ref, c_ref, s_ref, k, h, hi_ref, lo_ref)
    _cmatmul(hi_ref, lo_ref, whi_ref[...], wlo_ref[...], rf_ref)
    _stage1(hb_ref, c_ref, s_ref, k, h, hi_ref, lo_ref)
    _cmatmul(hi_ref, lo_ref, whi_ref[...], wlo_ref[...], rb_ref)

    def body(r0):
        fr, fi = _cresult(rf_ref, r0)
        br, bi = _cresult(rb_ref, r0)
        inv = inv_ref[pl.ds(r0, HY_RC), :]
        k_ref[pl.ds(r0, HY_RC), :] = jnp.concatenate([(fr + br) * inv, (fi - bi) * inv], axis=1)
    _row_chunks(body)


def hy_spectrum(hf_t, hb_t, inv_norm, L):
    n, n1, h, nk = _dft_plan(L)
    cs = _dft_consts(L)
    smem = pl.BlockSpec(memory_space=pltpu.SMEM)
    t_spec = pl.BlockSpec((h, HY_WIDTH, HY_N2), lambda k: (0, 0, 0))
    w_spec = pl.BlockSpec((None, HY_N2, 2 * HY_N2), lambda k: (k, 0, 0))
    return pl.pallas_call(
        functools.partial(_hy_spectrum_kernel, h=h),
        grid=(nk,),
        in_specs=[smem, smem, t_spec, t_spec, w_spec, w_spec,
                  pl.BlockSpec((HY_WIDTH, 1), lambda k: (0, 0))],
        out_specs=pl.BlockSpec((None, HY_WIDTH, 2 * HY_N2), lambda k: (k, 0, 0)),
        out_shape=jax.ShapeDtypeStruct((nk, HY_WIDTH, 2 * HY_N2), F32),
        scratch_shapes=_hy_scratch(2),
        compiler_params=_cparams(("parallel",), 48),
        name="hy_spectrum",
    )(cs["c1"], cs["s1"], hf_t, hb_t, cs["wf_hi"], cs["wf_lo"], inv_norm)


def _hy_scratch(n_results):
    return ([pltpu.VMEM((2 * W, HY_N2), BF16)] * 2
            + [pltpu.VMEM((2 * W, 2 * HY_N2), F32)] * n_results)


def _hy_conv_kernel(c_ref, s_ref, ic_ref, is_ref, m_ref, kf_ref, fhi_ref, flo_ref, ihi_ref, ilo_ref,
                    d_ref, o_ref, hi_ref, lo_ref, r_ref, *, h):
    k = pl.program_id(1)
    _stage1(m_ref, c_ref, s_ref, k, h, hi_ref, lo_ref)
    _cmatmul(hi_ref, lo_ref, fhi_ref[...], flo_ref[...], r_ref)

    def spectrum_product(r0):
        yr, yi = _cresult(r_ref, r0)
        kf = kf_ref[pl.ds(r0, HY_RC), :]
        kr, ki = kf[:, :HY_N2], kf[:, HY_N2:]
        _store_split(hi_ref, lo_ref, r0, yr * kr - yi * ki, yr * ki + yi * kr)
    _row_chunks(spectrum_product)
    _cmatmul(hi_ref, lo_ref, ihi_ref[...], ilo_ref[...], r_ref)

    @pl.when(k == 0)
    def _():
        def body(r0):
            br, bi = _cresult(r_ref, r0)
            d = d_ref[pl.ds(r0, HY_RC), :]
            for m in range(h):
                o_ref[m, pl.ds(r0, HY_RC), :] = (m_ref[m, pl.ds(r0, HY_RC), :] * d
                                                 + (ic_ref[m, 0] * br - is_ref[m, 0] * bi))
        _row_chunks(body)

    @pl.when(k > 0)
    def _():
        def body(r0):
            br, bi = _cresult(r_ref, r0)
            for m in range(h):
                o_ref[m, pl.ds(r0, HY_RC), :] += ic_ref[m, k] * br - is_ref[m, k] * bi
        _row_chunks(body)


def hy_conv(m_t, kf, d_col, L):
    n, n1, h, nk = _dft_plan(L)
    cs = _dft_consts(L)
    bsz = m_t.shape[0]
    smem = pl.BlockSpec(memory_space=pltpu.SMEM)
    t_spec = pl.BlockSpec((None, h, HY_WIDTH, HY_N2), lambda b, k: (b, 0, 0, 0))
    w_spec = pl.BlockSpec((None, HY_N2, 2 * HY_N2), lambda b, k: (k, 0, 0))
    return pl.pallas_call(
        functools.partial(_hy_conv_kernel, h=h),
        grid=(bsz, nk),
        in_specs=[smem, smem, smem, smem, t_spec,
                  pl.BlockSpec((None, HY_WIDTH, 2 * HY_N2), lambda b, k: (k, 0, 0)),
                  w_spec, w_spec, w_spec, w_spec,
                  pl.BlockSpec((HY_WIDTH, 1), lambda b, k: (0, 0))],
        out_specs=t_spec,
        out_shape=jax.ShapeDtypeStruct(m_t.shape, F32),
        scratch_shapes=_hy_scratch(1),
        compiler_params=_cparams(("parallel", "arbitrary"), 52),
        name="hy_conv",
    )(cs["c1"], cs["s1"], cs["ic"], cs["isn"], m_t, kf,
      cs["wf_hi"], cs["wf_lo"], cs["wi_hi"], cs["wi_lo"], d_col)


def hyena_mix(p, lp):
    L = p.shape[1]
    hf_t, hb_t, nrm = hy_filter_taps(L, lp)
    kf = hy_spectrum(hf_t, hb_t, (1.0 / nrm).reshape(HY_WIDTH, 1), L)
    m_t, x0 = hy_pre(p, lp["hy_conv_w"], lp["hy_conv_b"])
    z_t = hy_conv(m_t, kf, lp["hy_d"].astype(F32).reshape(HY_WIDTH, 1), L)
    return z_t, x0


def _gelu_tanh(x):
    return 0.5 * x * (1.0 + jnp.tanh(math.sqrt(2.0 / math.pi) * (x + 0.044715 * (x * x * x))))


def _merge_kernel(x_ref, y5_ref, zt_ref, x0_ref,
                  sx_ref, sxp_ref, sxn_ref, sb_ref, sc_ref, scp_ref, scn_ref,
                  g0a_ref, g0b_ref, g1a_ref, g1b_ref, g2a_ref, g2b_ref, mg_ref,
                  glu_ref, scw_ref, sco_ref, hyo_ref, ow_ref, o_ref):
    i = pl.program_id(1)
    first = i == 0
    last = i == pl.num_programs(1) - 1
    ag = _bdot(_gelu_tanh(y5_ref[...]), glu_ref[...])
    y_s5 = ag[:, :D_MODEL] * jax.nn.sigmoid(ag[:, D_MODEL:])
    l7 = slice(SUBLANES - 1, SUBLANES)
    cx = sc_ref[...] * sx_ref[...]
    cxp = scp_ref[l7, :] * sxp_ref[l7, :]
    cxn = scn_ref[0:1, :] * sxn_ref[0:1, :]
    y_sc = _bdot(sb_ref[...] * _conv3(cx, cxp, cxn, scw_ref[...], first, last), sco_ref[...])
    y_hy = _bdot(x0_ref[...] * zt_ref[...].T, hyo_ref[...])
    gate = lambda a, b: jax.nn.sigmoid(jnp.concatenate([a[...], b[...]], axis=1))
    m = gate(g0a_ref, g0b_ref) * y_s5 + gate(g1a_ref, g1b_ref) * y_sc + gate(g2a_ref, g2b_ref) * y_hy
    o_ref[...] = x_ref[...] + mg_ref[...] * _bdot(m, ow_ref[...])


def merge(x, p, y5, z_t, x0, mod3, wl, ctx):
    bsz, L, _ = x.shape
    tl = ROW_TILE
    row = lambda width, col: pl.BlockSpec((None, tl, width), lambda b, i: (b, i, col))
    full = lambda shape: pl.BlockSpec(shape, lambda b, i: (0,) * len(shape))
    c_sc = OFF_SC // SC_WIDTH
    half = D_MODEL // 2
    c_g = OFF_GATE // half
    specs = [row(D_MODEL, 0), row(S5_WIDTH, 0),
             pl.BlockSpec((None, None, HY_WIDTH, HY_N2), lambda b, i: (b, i, 0, 0)), row(HY_WIDTH, 0)]
    specs += _halo_specs(tl, L, SC_WIDTH, c_sc)
    specs += [row(SC_WIDTH, c_sc + 1)]
    specs += _halo_specs(tl, L, SC_WIDTH, c_sc + 2)
    specs += [row(half, c_g + k) for k in range(2 * N_BRANCH)] + [_mod_spec(2, ctx)]
    specs += [full((S5_WIDTH, 2 * D_MODEL)), full((3, SC_WIDTH)), full((SC_WIDTH, D_MODEL)),
              full((HY_WIDTH, D_MODEL)), full((D_MODEL, D_MODEL))]
    return pl.pallas_call(
        _merge_kernel,
        grid=(bsz, L // tl),
        in_specs=specs,
        out_specs=row(D_MODEL, 0),
        out_shape=jax.ShapeDtypeStruct((bsz, L, D_MODEL), F32),
        compiler_params=_cparams(("parallel", "parallel"), 48),
        name="merge",
    )(x, y5, z_t, x0, *([p] * 13), mod3,
      wl["s5_glu_w"], wl["sc_conv_w"], wl["sc_out_w"], wl["hy_out_w"], wl["out_w"])


def _mlp_kernel(x_ref, g_ref, sh_ref, sc_ref, mg_ref, w1_ref, w2_ref, fg_ref, o_ref, h_ref, acc_ref,
                *, final_norm):
    j = pl.program_id(2)

    @pl.when(j == 0)
    def _():
        h_ref[...] = _norm_mod(x_ref[...], g_ref[...], sh_ref[...], sc_ref[...]).astype(BF16)
        acc_ref[...] = jnp.zeros_like(acc_ref)

    r = jnp.maximum(jnp.dot(h_ref[...], w1_ref[...], preferred_element_type=F32), 0.0)
    acc_ref[...] += _bdot(r * r, w2_ref[...])

    @pl.when(j == pl.num_programs(2) - 1)
    def _():
        y = x_ref[...] + mg_ref[...] * acc_ref[...]
        if final_norm:
            y = (y * lax.rsqrt(jnp.mean(y * y, axis=-1, keepdims=True) + EPS)) * fg_ref[...]
        o_ref[...] = y


def mlp(x, g, mod3, w1, w2, final_g, ctx, final_norm):
    bsz, L, _ = x.shape
    tl = min(L, 1024)
    tf = 1024
    x_spec = pl.BlockSpec((None, tl, D_MODEL), lambda b, i, j: (b, i, 0))
    vec = pl.BlockSpec((1, D_MODEL), lambda b, i, j: (0, 0))
    return pl.pallas_call(
        functools.partial(_mlp_kernel, final_norm=final_norm),
        grid=(bsz, L // tl, D_FF // tf),
        in_specs=[x_spec, vec, _mod_spec(3, ctx), _mod_spec(4, ctx), _mod_spec(5, ctx),
                  pl.BlockSpec((D_MODEL, tf), lambda b, i, j: (0, j)),
                  pl.BlockSpec((tf, D_MODEL), lambda b, i, j: (j, 0)),
                  vec],
        out_specs=x_spec,
        out_shape=jax.ShapeDtypeStruct((bsz, L, D_MODEL), F32),
        scratch_shapes=[pltpu.VMEM((tl, D_MODEL), BF16), pltpu.VMEM((tl, D_MODEL), F32)],
        compiler_params=_cparams(("parallel", "parallel", "arbitrary"), 48),
        name="mlp",
    )(x, g.reshape(1, D_MODEL), mod3, mod3, mod3, w1, w2, final_g.reshape(1, D_MODEL))


def _sincos_2d(rows, cols, dim):
    quarter = dim // 4
    omega = 1.0 / (10000.0 ** (jnp.arange(quarter, dtype=F32) / quarter))
    er = jnp.arange(rows, dtype=F32)[:, None] * omega[None]
    ec = jnp.arange(cols, dtype=F32)[:, None] * omega[None]
    er = jnp.concatenate([jnp.sin(er), jnp.cos(er)], axis=-1)
    ec = jnp.concatenate([jnp.sin(ec), jnp.cos(ec)], axis=-1)
    emb = jnp.concatenate([
        jnp.broadcast_to(er[:, None, :], (rows, cols, dim // 2)),
        jnp.broadcast_to(ec[None, :, :], (rows, cols, dim // 2))], axis=-1)
    return emb.reshape(rows * cols, dim)


def kernel(x, c, ctx, c_ctx, ada_w, ada_b, norm1_g, norm2_g, w_in, s5_a_re, s5_a_im, s5_log_dt, s5_b_re, s5_b_im, s5_c_re, s5_c_im, s5_d, s5_glu_w, sc_conv_w, sc_out_w, hy_conv_w, hy_conv_b, hy_f_w1, hy_f_b1, hy_f_w2, hy_f_b2, hy_f_w3, hy_f_b3, hy_f_freq, hy_f_w4, hy_d, hy_out_w, out_w, mlp_w1, mlp_w2, final_g):
    bsz, L, _ = x.shape
    x = x + _sincos_2d(L // GRID_W, GRID_W, D_MODEL)[None]
    xc = ctx
    cc = jnp.concatenate([c, c_ctx[None, :], jnp.zeros((MOD_ROWS - bsz - 1, D_MODEL), F32)], axis=0)
    mod_all = ada_mod(cc, ada_w.astype(BF16), ada_b)
    for l in range(DEPTH):
        last = l == DEPTH - 1
        mod3 = mod_all[l].reshape(MOD_ROWS, 1, N_MOD * D_MODEL)
        w_in_l = w_in[l].astype(BF16)
        wl = {"s5_glu_w": s5_glu_w[l].astype(BF16), "sc_conv_w": sc_conv_w[l],
              "sc_out_w": sc_out_w[l].astype(BF16), "hy_out_w": hy_out_w[l].astype(BF16),
              "out_w": out_w[l].astype(BF16)}
        lp = {"hy_conv_w": hy_conv_w[l], "hy_conv_b": hy_conv_b[l],
              "hy_f_w1": hy_f_w1[l], "hy_f_b1": hy_f_b1[l], "hy_f_w2": hy_f_w2[l], "hy_f_b2": hy_f_b2[l],
              "hy_f_w3": hy_f_w3[l], "hy_f_b3": hy_f_b3[l], "hy_f_freq": hy_f_freq[l],
              "hy_f_w4": hy_f_w4[l], "hy_d": hy_d[l]}
        w1 = mlp_w1[l].astype(BF16)
        w2 = mlp_w2[l].astype(BF16)
        ops = _s5_operators(s5_a_re[l], s5_a_im[l], s5_log_dt[l], s5_b_re[l], s5_b_im[l],
                            s5_c_re[l], s5_c_im[l], s5_d[l])

        pc = norm_mm(xc, norm1_g[l], mod3, w_in_l[:, :S5_WIDTH] if last else w_in_l, ctx=True)
        yc5, ctx_final = s5_layer(pc[..., OFF_S5:OFF_S5 + S5_WIDTH], ops, None, not last)

        p = norm_mm(x, norm1_g[l], mod3, w_in_l, ctx=False)
        y5, _ = s5_layer(p[..., OFF_S5:OFF_S5 + S5_WIDTH], ops, ctx_final, True)
        x = merge(x, p, y5, *hyena_mix(p, lp), mod3, wl, ctx=False)
        x = mlp(x, norm2_g[l], mod3, w1, w2, final_g, ctx=False, final_norm=last)

        if not last:
            xc = merge(xc, pc, yc5, *hyena_mix(pc, lp), mod3, wl, ctx=True)
            xc = mlp(xc, norm2_g[l], mod3, w1, w2, final_g, ctx=True, final_norm=False)
    return x
```

```python
import functools
import math

import numpy as np
import jax
import jax.numpy as jnp
from jax import lax
from jax.experimental import pallas as pl
from jax.experimental.pallas import tpu as pltpu

F32 = jnp.float32
BF16 = jnp.bfloat16
HIGHEST = lax.Precision.HIGHEST

D_MODEL = 1024
DEPTH = 2
GRID_W = 64
S5_WIDTH = 512
S5_GROUP = 16
S5_GROUPS = S5_WIDTH // S5_GROUP
S5_STATE = 64
SC_WIDTH = 512
HY_WIDTH = 512
HY_BANDS = 16
HY_EMB = 1 + 2 * HY_BANDS
HY_FAST_DECAY = 0.3
HY_SLOW_DECAY = 1.5
HY_DECAY_TARGET = 1e-2
HY_DECAY_SHIFT = 0.05
HY_MAX_DECAY = math.log(HY_DECAY_TARGET) / HY_FAST_DECAY
HY_MIN_DECAY = math.log(HY_DECAY_TARGET) / HY_SLOW_DECAY
N_BRANCH = 3
D_FF = 4 * D_MODEL
N_MOD = 6
EPS = 1e-6
OFF_S5 = 0
OFF_SC = OFF_S5 + S5_WIDTH
OFF_HY = OFF_SC + 3 * SC_WIDTH
OFF_GATE = OFF_HY + 3 * HY_WIDTH
D_IN = OFF_GATE + N_BRANCH * D_MODEL

LANES = 128
SUBLANES = 8
S5_T = 8
S5_BLOCKS = S5_WIDTH // LANES
S5_HALF = 256
EMB_PAD = 40
MOD_ROWS = 8
CTX_ROW = 4
HY_N2 = 256
ROW_TILE = 256


def _cparams(sem, vmem_mb):
    return pltpu.CompilerParams(dimension_semantics=sem, vmem_limit_bytes=vmem_mb * 1024 * 1024)


def _bdot(a, b):
    return jnp.dot(a.astype(BF16), b.astype(BF16), preferred_element_type=F32)


def _hdot(a, b):
    return jnp.dot(a, b, precision=HIGHEST, preferred_element_type=F32)


def _ada_kernel(c_ref, w_ref, b_ref, o_ref):
    c = c_ref[...]
    s = c * jax.nn.sigmoid(c)
    o_ref[...] = _bdot(s, w_ref[...]) + b_ref[...]


def ada_mod(cc, ada_w, ada_b):
    n = N_MOD * D_MODEL
    tn = 1536
    return pl.pallas_call(
        _ada_kernel,
        grid=(DEPTH, n // tn),
        in_specs=[
            pl.BlockSpec((MOD_ROWS, D_MODEL), lambda l, j: (0, 0)),
            pl.BlockSpec((None, D_MODEL, tn), lambda l, j: (l, 0, j)),
            pl.BlockSpec((None, 1, tn), lambda l, j: (l, 0, j)),
        ],
        out_specs=pl.BlockSpec((None, MOD_ROWS, tn), lambda l, j: (l, 0, j)),
        out_shape=jax.ShapeDtypeStruct((DEPTH, MOD_ROWS, n), F32),
        compiler_params=_cparams(("parallel", "parallel"), 40),
        name="ada_mod",
    )(cc, ada_w, ada_b.reshape(DEPTH, 1, n))


def _mod_spec(chunk, ctx):
    if ctx:
        return pl.BlockSpec((None, 1, D_MODEL), lambda b, *_: (CTX_ROW, 0, chunk))
    return pl.BlockSpec((None, 1, D_MODEL), lambda b, *_: (b, 0, chunk))


def _norm_mod(x, g, sh, sc):
    y = x * lax.rsqrt(jnp.mean(x * x, axis=-1, keepdims=True) + EPS)
    return (y * g) * (1.0 + sc) + sh


def _norm_mm_kernel(x_ref, g_ref, sh_ref, sc_ref, w_ref, o_ref, h_ref):
    @pl.when(pl.program_id(2) == 0)
    def _():
        h_ref[...] = _norm_mod(x_ref[...], g_ref[...], sh_ref[...], sc_ref[...]).astype(BF16)

    o_ref[...] = jnp.dot(h_ref[...], w_ref[...], preferred_element_type=F32)


def norm_mm(x, g, mod3, w, ctx):
    bsz, L, _ = x.shape
    n = w.shape[1]
    tl = min(L, 1024)
    tn = n // 4 if n % (4 * LANES) == 0 and n > 2048 else n
    return pl.pallas_call(
        _norm_mm_kernel,
        grid=(bsz, L // tl, n // tn),
        in_specs=[
            pl.BlockSpec((None, tl, D_MODEL), lambda b, i, j: (b, i, 0)),
            pl.BlockSpec((1, D_MODEL), lambda b, i, j: (0, 0)),
            _mod_spec(0, ctx),
            _mod_spec(1, ctx),
            pl.BlockSpec((D_MODEL, tn), lambda b, i, j: (0, j)),
        ],
        out_specs=pl.BlockSpec((None, tl, tn), lambda b, i, j: (b, i, j)),
        out_shape=jax.ShapeDtypeStruct((bsz, L, n), F32),
        scratch_shapes=[pltpu.VMEM((tl, D_MODEL), BF16)],
        compiler_params=_cparams(("parallel", "parallel", "arbitrary"), 48),
        name="norm_mm",
    )(x, g.reshape(1, D_MODEL), mod3, mod3, w)


def _s5_operators(a_re, a_im, log_dt, b_re, b_im, c_re, c_im, d_skip):
    T = S5_T
    G, N, P = S5_GROUPS, S5_STATE, S5_GROUP
    taus = jnp.arange(T + 1, dtype=F32)[:, None, None]
    ops = []
    for k in range(2):
        ar, ai = a_re[k].astype(F32), a_im[k].astype(F32)
        dt = jnp.exp(log_dt[k].astype(F32))[:, None]
        mag = jnp.exp(ar * dt * taus)
        ang = ai * dt * taus
        pr, pi = mag * jnp.cos(ang), mag * jnp.sin(ang)
        den = ar * ar + ai * ai
        nr, ni = pr[1] - 1.0, pi[1]
        f_re = (nr * ar + ni * ai) / den
        f_im = (ni * ar - nr * ai) / den
        br, bi = b_re[k].astype(F32), b_im[k].astype(F32)
        bbr = f_re[..., None] * br - f_im[..., None] * bi
        bbi = f_re[..., None] * bi + f_im[..., None] * br
        er = pr[..., None] * bbr - pi[..., None] * bbi
        ei = pr[..., None] * bbi + pi[..., None] * bbr
        cr, ci = c_re[k].astype(F32), c_im[k].astype(F32)
        kk = (jnp.einsum("gqn,tgnp->tgqp", cr, er[:T], precision=HIGHEST)
              - jnp.einsum("gqn,tgnp->tgqp", ci, ei[:T], precision=HIGHEST))
        car = cr[None] * pr[:, :, None, :] - ci[None] * pi[:, :, None, :]
        cai = cr[None] * pi[:, :, None, :] + ci[None] * pr[:, :, None, :]
        ops.append((pr, pi, er, ei, kk, car, cai))

    j_in = jnp.arange(T)[:, None]
    j_out = jnp.arange(T)[None, :]
    tau = (j_out - j_in)[..., None, None, None]
    dterm = jnp.eye(P, dtype=F32)[None] * d_skip.astype(F32).reshape(G, 1, P)
    ksum = (jnp.where(tau >= 0, ops[0][4][jnp.clip(j_out - j_in, 0, T - 1)], 0.0)
            + jnp.where(tau <= 0, ops[1][4][jnp.clip(j_in - j_out, 0, T - 1)], 0.0)
            + jnp.where(tau == 0, dterm[None, None], 0.0))
    ks = ksum.reshape(T, T, S5_BLOCKS, 8, P, P).transpose(2, 0, 3, 5, 1, 4).astype(BF16)
    eye8 = jnp.eye(8, dtype=BF16)
    m_op = (ks[:, :, :, :, :, None, :] * eye8[None, None, :, None, None, :, None]
            ).reshape(S5_BLOCKS, T * LANES, T * LANES)

    lane = jnp.arange(2 * S5_HALF)
    grp_half = (jnp.arange(8)[None, :, None]
                == jnp.arange(2)[:, None, None] * 4 + ((lane % S5_HALF) // N)[None, None, :]).astype(BF16)
    grp_all = jnp.concatenate([grp_half[0], grp_half[1]], axis=-1)

    def b_mat(er, ei, sel):
        def one(e):
            e6 = e[sel].reshape(T, S5_BLOCKS, 2, 4, N, P)
            return e6.transpose(1, 2, 0, 5, 3, 4).reshape(S5_BLOCKS, 2, T, P, S5_HALF)
        v = jnp.concatenate([one(er), one(ei)], axis=-1).astype(BF16)
        full = v[:, :, :, None, :, :] * grp_half[None, :, None, :, None, :]
        return full.reshape(S5_BLOCKS, 2, T * LANES, 2 * S5_HALF)

    def c_mat_t(car, cai, sel):
        def one(c):
            c6 = c[sel].reshape(T, S5_BLOCKS, 2, 4, P, N)
            return c6.transpose(1, 0, 4, 2, 3, 5).reshape(S5_BLOCKS, T, P, 2, S5_HALF)
        v = jnp.stack([one(car), -one(cai)], axis=4).reshape(S5_BLOCKS, T, P, 4 * S5_HALF).astype(BF16)
        full = v[:, :, None, :, :] * grp_all[None, None, :, None, :]
        return full.reshape(S5_BLOCKS, T * LANES, 4 * S5_HALF)

    jj = jnp.arange(T)
    b_f = b_mat(ops[0][2], ops[0][3], T - 1 - jj)
    b_r = b_mat(ops[1][2], ops[1][3], jj)
    c_f = c_mat_t(ops[0][5], ops[0][6], jj + 1)
    c_r = c_mat_t(ops[1][5], ops[1][6], T - jj)

    def at(pr, pi):
        r = pr[T].reshape(S5_BLOCKS, 2, 1, S5_HALF)
        i = pi[T].reshape(S5_BLOCKS, 2, 1, S5_HALF)
        return jnp.concatenate([r, i], axis=-1)

    return dict(m=m_op, b_f=b_f, b_r=b_r, c_f=c_f, c_r=c_r,
                at_f=at(ops[0][0], ops[0][1]), at_r=at(ops[1][0], ops[1][1]))


def _cmul_add(a, s, d):
    h = S5_HALF
    ar, ai = a[:, :h], a[:, h:]
    sr, si = s[:, :h], s[:, h:]
    return jnp.concatenate([ar * sr - ai * si, ar * si + ai * sr], axis=1) + d


def _s5_scan_kernel(x_ref, bf_ref, br_ref, atf_ref, atr_ref, if_ref, ir_ref,
                    s_ref, r_ref, ff_ref, fr_ref, *, rows):
    rc = min(rows, 256)
    for r0 in range(0, rows, rc):
        xs = x_ref[r0:r0 + rc, :]
        s_ref[r0:r0 + rc, :] = jnp.dot(xs, bf_ref[...], preferred_element_type=F32)
        r_ref[r0:r0 + rc, :] = jnp.dot(xs, br_ref[...], preferred_element_type=F32)

    ntile = rows // SUBLANES
    atf = atf_ref[...]
    atr = atr_ref[...]
    low = lax.broadcasted_iota(jnp.int32, (SUBLANES, 2 * S5_HALF), 0) < 4

    def body(i, carry):
        s, r = carry
        fo = pl.multiple_of(i * SUBLANES, SUBLANES)
        d = s_ref[pl.ds(fo, SUBLANES), :]
        t1 = pltpu.roll(_cmul_add(atf, s, d), 4, 0)
        t2 = _cmul_add(atf, t1, d)
        s_ref[pl.ds(fo, SUBLANES), :] = jnp.where(low, s, t1)
        ro = pl.multiple_of((ntile - 1 - i) * SUBLANES, SUBLANES)
        e = r_ref[pl.ds(ro, SUBLANES), :]
        u1 = pltpu.roll(_cmul_add(atr, r, e), 4, 0)
        u2 = _cmul_add(atr, u1, e)
        r_ref[pl.ds(ro, SUBLANES), :] = jnp.where(low, u1, r)
        return pltpu.roll(t2, 4, 0), pltpu.roll(u2, 4, 0)

    s0 = if_ref[...]
    r0 = pltpu.roll(ir_ref[...], 4, 0)
    s, r = lax.fori_loop(0, ntile, body, (s0, r0))
    ff_ref[...] = s
    fr_ref[...] = pltpu.roll(r, 4, 0)


def s5_scan(x4, ops, init_f, init_r):
    nb, rows, tk = x4.shape
    w = 2 * S5_HALF
    kern = functools.partial(_s5_scan_kernel, rows=rows)
    st_spec = pl.BlockSpec((None, None, SUBLANES, w), lambda b, h: (b, h, 0, 0))
    out_shape = (
        jax.ShapeDtypeStruct((nb, rows, 2 * w), F32),
        jax.ShapeDtypeStruct((nb, rows, 2 * w), F32),
        jax.ShapeDtypeStruct((nb, 2, SUBLANES, w), F32),
        jax.ShapeDtypeStruct((nb, 2, SUBLANES, w), F32),
    )
    return pl.pallas_call(
        kern,
        grid=(nb, 2),
        in_specs=[
            pl.BlockSpec((None, rows, tk), lambda b, h: (b, 0, 0)),
            pl.BlockSpec((None, None, tk, w), lambda b, h: (b, h, 0, 0)),
            pl.BlockSpec((None, None, tk, w), lambda b, h: (b, h, 0, 0)),
            pl.BlockSpec((None, None, 1, w), lambda b, h: (b, h, 0, 0)),
            pl.BlockSpec((None, None, 1, w), lambda b, h: (b, h, 0, 0)),
            st_spec, st_spec,
        ],
        out_specs=(
            pl.BlockSpec((None, rows, w), lambda b, h: (b, 0, h)),
            pl.BlockSpec((None, rows, w), lambda b, h: (b, 0, h)),
            st_spec, st_spec,
        ),
        out_shape=out_shape,
        compiler_params=_cparams(("parallel", "parallel"), 48),
        name="s5_scan",
    )(x4, ops["b_f"], ops["b_r"], ops["at_f"], ops["at_r"], init_f, init_r)


def _s5_out_kernel(x_ref, s_ref, r_ref, m_ref, cf_ref, cr_ref, o_ref):
    nt = (((1,), (1,)), ((), ()))
    acc = jnp.dot(x_ref[...], m_ref[...], preferred_element_type=F32)
    acc += lax.dot_general(s_ref[...].astype(BF16), cf_ref[...], nt, preferred_element_type=F32)
    acc += lax.dot_general(r_ref[...].astype(BF16), cr_ref[...], nt, preferred_element_type=F32)
    o_ref[...] = acc


def s5_out(x4, s_st, r_st, ops):
    nb, rows, tk = x4.shape
    tr = min(rows, 512)
    w = 4 * S5_HALF
    return pl.pallas_call(
        _s5_out_kernel,
        grid=(nb, rows // tr),
        in_specs=[
            pl.BlockSpec((None, tr, tk), lambda b, i: (b, i, 0)),
            pl.BlockSpec((None, tr, w), lambda b, i: (b, i, 0)),
            pl.BlockSpec((None, tr, w), lambda b, i: (b, i, 0)),
            pl.BlockSpec((None, tk, tk), lambda b, i: (b, 0, 0)),
            pl.BlockSpec((None, tk, w), lambda b, i: (b, 0, 0)),
            pl.BlockSpec((None, tk, w), lambda b, i: (b, 0, 0)),
        ],
        out_specs=pl.BlockSpec((None, tr, tk), lambda b, i: (b, i, 0)),
        out_shape=jax.ShapeDtypeStruct((nb, rows, tk), F32),
        compiler_params=_cparams(("parallel", "parallel"), 48),
        name="s5_out",
    )(x4, s_st, r_st, ops["m"], ops["c_f"], ops["c_r"])


def _to_chunk_rows(u):
    bsz, L, _ = u.shape
    nc = L // S5_T
    x = u.astype(BF16).reshape(bsz, nc, S5_T, S5_BLOCKS, LANES)
    return x.transpose(3, 1, 0, 2, 4).reshape(S5_BLOCKS, nc * bsz, S5_T * LANES)


def _from_chunk_rows(y4, bsz):
    nb, rows, _ = y4.shape
    nc = rows // bsz
    y = y4.reshape(nb, nc, bsz, S5_T, LANES).transpose(2, 1, 3, 0, 4)
    return y.reshape(bsz, nc * S5_T, S5_WIDTH)


def s5_layer(u, ops, init, readout):
    bsz = u.shape[0]
    x4 = _to_chunk_rows(u)
    if init is None:
        z = jnp.zeros((S5_BLOCKS, 2, SUBLANES, 2 * S5_HALF), F32)
        init = (z, z)
    s_st, r_st, fin_f, fin_r = s5_scan(x4, ops, init[0], init[1])
    y = _from_chunk_rows(s5_out(x4, s_st, r_st, ops), bsz) if readout else None
    return y, (fin_f, fin_r)


def _conv3(x, prev_row, next_row, w, first, last):
    n = x.shape[0]
    rows = lax.broadcasted_iota(jnp.int32, x.shape, 0)
    pz = jnp.where(first, 0.0, prev_row)
    nz = jnp.where(last, 0.0, next_row)
    xp = jnp.where(rows == 0, pz, pltpu.roll(x, 1, 0))
    xn = jnp.where(rows == n - 1, nz, pltpu.roll(x, n - 1, 0))
    return w[0:1, :] * xp + w[1:2, :] * x + w[2:3, :] * xn


def _halo_specs(tl, L, width, col):
    tb = tl // SUBLANES
    nb = L // SUBLANES
    cur = pl.BlockSpec((None, tl, width), lambda b, i: (b, i, col))
    prv = pl.BlockSpec((None, SUBLANES, width), lambda b, i: (b, jnp.maximum(i * tb - 1, 0), col))
    nxt = pl.BlockSpec((None, SUBLANES, width), lambda b, i: (b, jnp.minimum((i + 1) * tb, nb - 1), col))
    return [cur, prv, nxt]


def _dft_plan(L):
    n = 2 * L
    n1 = n // HY_N2
    return n, n1, n1 // 2, n1 // 2 + 1


def _split_bf16(a):
    hi = np.asarray(a, np.float32).astype(BF16)
    lo = (np.asarray(a, np.float32) - hi.astype(np.float32)).astype(BF16)
    return hi, lo


@functools.lru_cache(maxsize=None)
def _dft_consts(L):
    n, n1, h, nk = _dft_plan(L)
    k1 = np.arange(nk)[:, None]
    m1 = np.arange(h)[None, :]
    th = 2.0 * np.pi * k1 * m1 / n1
    wt = np.where((k1 == 0) | (k1 == h), 1.0, 2.0) / n
    k2 = np.arange(HY_N2)[:, None]
    m2 = np.arange(HY_N2)[None, :]
    ph = -2.0 * np.pi * (k2 * m2 / HY_N2)[None] - 2.0 * np.pi * np.arange(nk)[:, None, None] * m2[None] / n
    gr, gi = np.cos(ph), np.sin(ph)
    wf = np.concatenate([gr.transpose(0, 2, 1), gi.transpose(0, 2, 1)], axis=2)
    wi = np.concatenate([gr, -gi], axis=2)
    f = lambda a: np.asarray(a, np.float32)
    wf_hi, wf_lo = _split_bf16(wf)
    wi_hi, wi_lo = _split_bf16(wi)
    return dict(c1=f(np.cos(th)), s1=f(-np.sin(th)),
                ic=f((wt * np.cos(th)).T), isn=f((wt * np.sin(th)).T),
                wf_hi=wf_hi, wf_lo=wf_lo, wi_hi=wi_hi, wi_lo=wi_lo)


HY_RC = 32
W = HY_WIDTH


def _row_chunks(body):
    def step(ci, carry):
        body(pl.multiple_of(ci * HY_RC, HY_RC))
        return carry
    lax.fori_loop(0, W // HY_RC, step, 0)


def _store_split(hi_ref, lo_ref, r0, vr, vi):
    for off, v in ((0, vr), (W, vi)):
        hi = v.astype(BF16)
        hi_ref[pl.ds(off + r0, HY_RC), :] = hi
        lo_ref[pl.ds(off + r0, HY_RC), :] = (v - hi.astype(F32)).astype(BF16)


def _stage1(x_ref, c_ref, s_ref, k, h, hi_ref, lo_ref):
    def body(r0):
        x0 = x_ref[0, pl.ds(r0, HY_RC), :]
        ar = c_ref[k, 0] * x0
        ai = s_ref[k, 0] * x0
        for m in range(1, h):
            xm = x_ref[m, pl.ds(r0, HY_RC), :]
            ar += c_ref[k, m] * xm
            ai += s_ref[k, m] * xm
        _store_split(hi_ref, lo_ref, r0, ar, ai)
    _row_chunks(body)


def _cmatmul(hi_ref, lo_ref, w_hi, w_lo, r_ref):
    a_hi = hi_ref[...]
    r_ref[...] = (jnp.dot(a_hi, w_hi, preferred_element_type=F32)
                  + jnp.dot(lo_ref[...], w_hi, preferred_element_type=F32)
                  + jnp.dot(a_hi, w_lo, preferred_element_type=F32))


def _cresult(r_ref, r0):
    top = r_ref[pl.ds(r0, HY_RC), :]
    bot = r_ref[pl.ds(W + r0, HY_RC), :]
    return top[:, :HY_N2] - bot[:, HY_N2:], top[:, HY_N2:] + bot[:, :HY_N2]


def _hy_pre_kernel(v_ref, vp_ref, vn_ref, a_ref, ap_ref, an_ref, z_ref, zp_ref, zn_ref,
                   w_ref, b_ref, mt_ref, x0_ref):
    i = pl.program_id(1)
    first = i == 0
    last = i == pl.num_programs(1) - 1
    w = w_ref[...]
    b = b_ref[...]

    def part(c, p, n, k):
        sl = slice(k * HY_WIDTH, (k + 1) * HY_WIDTH)
        return _conv3(c[...], p[SUBLANES - 1:SUBLANES, :], n[0:1, :], w[:, sl], first, last) + b[:, sl]

    v = part(v_ref, vp_ref, vn_ref, 0)
    x1 = part(a_ref, ap_ref, an_ref, 1)
    mt_ref[...] = (x1 * v).T
    x0_ref[...] = part(z_ref, zp_ref, zn_ref, 2)


def hy_pre(p, conv_w, conv_b):
    bsz, L, _ = p.shape
    tl = ROW_TILE
    c0 = OFF_HY // HY_WIDTH
    specs = []
    for k in range(3):
        specs += _halo_specs(tl, L, HY_WIDTH, c0 + k)
    specs += [pl.BlockSpec((3, 3 * HY_WIDTH), lambda b, i: (0, 0)),
              pl.BlockSpec((1, 3 * HY_WIDTH), lambda b, i: (0, 0))]
    return pl.pallas_call(
        _hy_pre_kernel,
        grid=(bsz, L // tl),
        in_specs=specs,
        out_specs=(pl.BlockSpec((None, None, HY_WIDTH, HY_N2), lambda b, i: (b, i, 0, 0)),
                   pl.BlockSpec((None, tl, HY_WIDTH), lambda b, i: (b, i, 0))),
        out_shape=(jax.ShapeDtypeStruct((bsz, L // HY_N2, HY_WIDTH, HY_N2), F32),
                   jax.ShapeDtypeStruct((bsz, L, HY_WIDTH), F32)),
        compiler_params=_cparams(("parallel", "parallel"), 32),
        name="hy_pre",
    )(*([p] * 9), conv_w, conv_b.reshape(1, -1))


def _hy_filter_kernel(z_ref, w1_ref, b1_ref, w2_ref, b2_ref, w3_ref, b3_ref, fr_ref, w4_ref,
                      dl_ref, hf_ref, hb_ref, nrm_ref, *, tl):
    i = pl.program_id(0)
    z = z_ref[...]
    fr = fr_ref[...]
    h = jnp.sin(fr * (_hdot(z, w1_ref[...]) + b1_ref[...]))
    h = jnp.sin(fr * (_hdot(h, w2_ref[...]) + b2_ref[...]))
    h = jnp.sin(fr * (_hdot(h, w3_ref[...]) + b3_ref[...]))
    h = _hdot(h, w4_ref[...])
    t = z[:, 0:1]
    window = jnp.exp(-t * dl_ref[...]) + HY_DECAY_SHIFT
    hf = h[:, :HY_WIDTH] * window
    rows = lax.broadcasted_iota(jnp.int32, (tl, HY_WIDTH), 0) + i * tl
    hb = jnp.where(rows == 0, 0.0, h[:, HY_WIDTH:] * window)
    hf_ref[...] = hf.T
    hb_ref[...] = hb.T
    part = jnp.sum(jnp.abs(hf) + jnp.abs(hb), axis=0, keepdims=True)

    @pl.when(i == 0)
    def _():
        nrm_ref[...] = jnp.zeros_like(nrm_ref)

    nrm_ref[...] += part


def hy_filter_taps(L, lp):
    t = jnp.linspace(0.0, 1.0, L, dtype=F32)[:, None]
    ang = 2.0 * math.pi * jnp.arange(L, dtype=F32)[:, None] / L
    bands = jnp.linspace(1e-4, HY_BANDS - 1, HY_BANDS, dtype=F32)[None, :]
    z = jnp.concatenate([t, jnp.cos(bands * ang), -jnp.sin(bands * ang),
                         jnp.zeros((L, EMB_PAD - HY_EMB), F32)], axis=-1)
    w1 = jnp.concatenate([lp["hy_f_w1"], jnp.zeros((EMB_PAD - HY_EMB, lp["hy_f_w1"].shape[1]), F32)], axis=0)
    deltas = jnp.abs(jnp.linspace(HY_MIN_DECAY, HY_MAX_DECAY, HY_WIDTH, dtype=F32))[None, :]
    tl = HY_N2
    hid = w1.shape[1]
    full = lambda shape: pl.BlockSpec(shape, lambda i: (0,) * len(shape))
    o_spec = pl.BlockSpec((None, HY_WIDTH, HY_N2), lambda i: (i, 0, 0))
    shp = jax.ShapeDtypeStruct((L // HY_N2, HY_WIDTH, HY_N2), F32)
    return pl.pallas_call(
        functools.partial(_hy_filter_kernel, tl=tl),
        grid=(L // tl,),
        in_specs=[pl.BlockSpec((tl, EMB_PAD), lambda i: (i, 0)),
                  full((EMB_PAD, hid)), full((1, hid)), full((hid, hid)), full((1, hid)),
                  full((hid, hid)), full((1, hid)), full((1, hid)), full((hid, 2 * HY_WIDTH)),
                  full((1, HY_WIDTH))],
        out_specs=(o_spec, o_spec, pl.BlockSpec((1, HY_WIDTH), lambda i: (0, 0))),
        out_shape=(shp, shp, jax.ShapeDtypeStruct((1, HY_WIDTH), F32)),
        compiler_params=_cparams(("arbitrary",), 32),
        name="hy_filter_taps",
    )(z, w1, lp["hy_f_b1"].reshape(1, -1), lp["hy_f_w2"], lp["hy_f_b2"].reshape(1, -1),
      lp["hy_f_w3"], lp["hy_f_b3"].reshape(1, -1), lp["hy_f_freq"].reshape(1, -1), lp["hy_f_w4"], deltas)


def _hy_spectrum_kernel(c_ref, s_ref, hf_ref, hb_ref, whi_ref, wlo_ref, inv_ref, k_ref,
                        hi_ref, lo_ref, rf_ref, rb_ref, *, h):
    k = pl.program_id(0)
    _stage1(hf_ref, c_ref, s_ref, k, h, hi_ref, lo_ref)
    _cmatmul(hi_ref, lo_ref, whi_ref[...], wlo_ref[...], rf_ref)
    _stage1(hb_ref, c_ref, s_ref, k, h, hi_ref, lo_ref)
    _cmatmul(hi_ref, lo_ref, whi_ref[...], wlo_ref[...], rb_ref)

    def body(r0):
        fr, fi = _cresult(rf_ref, r0)
        br, bi = _cresult(rb_ref, r0)
        inv = inv_ref[pl.ds(r0, HY_RC), :]
        k_ref[pl.ds(r0, HY_RC), :] = jnp.concatenate([(fr + br) * inv, (fi - bi) * inv], axis=1)
    _row_chunks(body)


def hy_spectrum(hf_t, hb_t, inv_norm, L):
    n, n1, h, nk = _dft_plan(L)
    cs = _dft_consts(L)
    smem = pl.BlockSpec(memory_space=pltpu.SMEM)
    t_spec = pl.BlockSpec((h, HY_WIDTH, HY_N2), lambda k: (0, 0, 0))
    w_spec = pl.BlockSpec((None, HY_N2, 2 * HY_N2), lambda k: (k, 0, 0))
    return pl.pallas_call(
        functools.partial(_hy_spectrum_kernel, h=h),
        grid=(nk,),
        in_specs=[smem, smem, t_spec, t_spec, w_spec, w_spec,
                  pl.BlockSpec((HY_WIDTH, 1), lambda k: (0, 0))],
        out_specs=pl.BlockSpec((None, HY_WIDTH, 2 * HY_N2), lambda k: (k, 0, 0)),
        out_shape=jax.ShapeDtypeStruct((nk, HY_WIDTH, 2 * HY_N2), F32),
        scratch_shapes=_hy_scratch(2),
        compiler_params=_cparams(("parallel",), 48),
        name="hy_spectrum",
    )(cs["c1"], cs["s1"], hf_t, hb_t, cs["wf_hi"], cs["wf_lo"], inv_norm)


def _hy_scratch(n_results):
    return ([pltpu.VMEM((2 * W, HY_N2), BF16)] * 2
            + [pltpu.VMEM((2 * W, 2 * HY_N2), F32)] * n_results)


def _hy_conv_kernel(c_ref, s_ref, ic_ref, is_ref, m_ref, kf_ref, fhi_ref, flo_ref, ihi_ref, ilo_ref,
                    d_ref, o_ref, hi_ref, lo_ref, r_ref, *, h):
    k = pl.program_id(1)
    _stage1(m_ref, c_ref, s_ref, k, h, hi_ref, lo_ref)
    _cmatmul(hi_ref, lo_ref, fhi_ref[...], flo_ref[...], r_ref)

    def spectrum_product(r0):
        yr, yi = _cresult(r_ref, r0)
        kf = kf_ref[pl.ds(r0, HY_RC), :]
        kr, ki = kf[:, :HY_N2], kf[:, HY_N2:]
        _store_split(hi_ref, lo_ref, r0, yr * kr - yi * ki, yr * ki + yi * kr)
    _row_chunks(spectrum_product)
    _cmatmul(hi_ref, lo_ref, ihi_ref[...], ilo_ref[...], r_ref)

    @pl.when(k == 0)
    def _():
        def body(r0):
            br, bi = _cresult(r_ref, r0)
            d = d_ref[pl.ds(r0, HY_RC), :]
            for m in range(h):
                o_ref[m, pl.ds(r0, HY_RC), :] = (m_ref[m, pl.ds(r0, HY_RC), :] * d
                                                 + (ic_ref[m, 0] * br - is_ref[m, 0] * bi))
        _row_chunks(body)

    @pl.when(k > 0)
    def _():
        def body(r0):
            br, bi = _cresult(r_ref, r0)
            for m in range(h):
                o_ref[m, pl.ds(r0, HY_RC), :] += ic_ref[m, k] * br - is_ref[m, k] * bi
        _row_chunks(body)


def hy_conv(m_t, kf, d_col, L):
    n, n1, h, nk = _dft_plan(L)
    cs = _dft_consts(L)
    bsz = m_t.shape[0]
    smem = pl.BlockSpec(memory_space=pltpu.SMEM)
    t_spec = pl.BlockSpec((None, h, HY_WIDTH, HY_N2), lambda b, k: (b, 0, 0, 0))
    w_spec = pl.BlockSpec((None, HY_N2, 2 * HY_N2), lambda b, k: (k, 0, 0))
    return pl.pallas_call(
        functools.partial(_hy_conv_kernel, h=h),
        grid=(bsz, nk),
        in_specs=[smem, smem, smem, smem, t_spec,
                  pl.BlockSpec((None, HY_WIDTH, 2 * HY_N2), lambda b, k: (k, 0, 0)),
                  w_spec, w_spec, w_spec, w_spec,
                  pl.BlockSpec((HY_WIDTH, 1), lambda b, k: (0, 0))],
        out_specs=t_spec,
        out_shape=jax.ShapeDtypeStruct(m_t.shape, F32),
        scratch_shapes=_hy_scratch(1),
        compiler_params=_cparams(("parallel", "arbitrary"), 52),
        name="hy_conv",
    )(cs["c1"], cs["s1"], cs["ic"], cs["isn"], m_t, kf,
      cs["wf_hi"], cs["wf_lo"], cs["wi_hi"], cs["wi_lo"], d_col)


def hyena_mix(p, lp):
    L = p.shape[1]
    hf_t, hb_t, nrm = hy_filter_taps(L, lp)
    kf = hy_spectrum(hf_t, hb_t, (1.0 / nrm).reshape(HY_WIDTH, 1), L)
    m_t, x0 = hy_pre(p, lp["hy_conv_w"], lp["hy_conv_b"])
    z_t = hy_conv(m_t, kf, lp["hy_d"].astype(F32).reshape(HY_WIDTH, 1), L)
    return z_t, x0


def _gelu_tanh(x):
    return 0.5 * x * (1.0 + jnp.tanh(math.sqrt(2.0 / math.pi) * (x + 0.044715 * (x * x * x))))


def _merge_kernel(x_ref, y5_ref, zt_ref, x0_ref,
                  sx_ref, sxp_ref, sxn_ref, sb_ref, sc_ref, scp_ref, scn_ref,
                  g0a_ref, g0b_ref, g1a_ref, g1b_ref, g2a_ref, g2b_ref, mg_ref,
                  glu_ref, scw_ref, sco_ref, hyo_ref, ow_ref, o_ref):
    i = pl.program_id(1)
    first = i == 0
    last = i == pl.num_programs(1) - 1
    ag = _bdot(_gelu_tanh(y5_ref[...]), glu_ref[...])
    y_s5 = ag[:, :D_MODEL] * jax.nn.sigmoid(ag[:, D_MODEL:])
    l7 = slice(SUBLANES - 1, SUBLANES)
    cx = sc_ref[...] * sx_ref[...]
    cxp = scp_ref[l7, :] * sxp_ref[l7, :]
    cxn = scn_ref[0:1, :] * sxn_ref[0:1, :]
    y_sc = _bdot(sb_ref[...] * _conv3(cx, cxp, cxn, scw_ref[...], first, last), sco_ref[...])
    y_hy = _bdot(x0_ref[...] * zt_ref[...].T, hyo_ref[...])
    gate = lambda a, b: jax.nn.sigmoid(jnp.concatenate([a[...], b[...]], axis=1))
    m = gate(g0a_ref, g0b_ref) * y_s5 + gate(g1a_ref, g1b_ref) * y_sc + gate(g2a_ref, g2b_ref) * y_hy
    o_ref[...] = x_ref[...] + mg_ref[...] * _bdot(m, ow_ref[...])


def merge(x, p, y5, z_t, x0, mod3, wl, ctx):
    bsz, L, _ = x.shape
    tl = ROW_TILE
    row = lambda width, col: pl.BlockSpec((None, tl, width), lambda b, i: (b, i, col))
    full = lambda shape: pl.BlockSpec(shape, lambda b, i: (0,) * len(shape))
    c_sc = OFF_SC // SC_WIDTH
    half = D_MODEL // 2
    c_g = OFF_GATE // half
    specs = [row(D_MODEL, 0), row(S5_WIDTH, 0),
             pl.BlockSpec((None, None, HY_WIDTH, HY_N2), lambda b, i: (b, i, 0, 0)), row(HY_WIDTH, 0)]
    specs += _halo_specs(tl, L, SC_WIDTH, c_sc)
    specs += [row(SC_WIDTH, c_sc + 1)]
    specs += _halo_specs(tl, L, SC_WIDTH, c_sc + 2)
    specs += [row(half, c_g + k) for k in range(2 * N_BRANCH)] + [_mod_spec(2, ctx)]
    specs += [full((S5_WIDTH, 2 * D_MODEL)), full((3, SC_WIDTH)), full((SC_WIDTH, D_MODEL)),
              full((HY_WIDTH, D_MODEL)), full((D_MODEL, D_MODEL))]
    return pl.pallas_call(
        _merge_kernel,
        grid=(bsz, L // tl),
        in_specs=specs,
        out_specs=row(D_MODEL, 0),
        out_shape=jax.ShapeDtypeStruct((bsz, L, D_MODEL), F32),
        compiler_params=_cparams(("parallel", "parallel"), 48),
        name="merge",
    )(x, y5, z_t, x0, *([p] * 13), mod3,
      wl["s5_glu_w"], wl["sc_conv_w"], wl["sc_out_w"], wl["hy_out_w"], wl["out_w"])


def _mlp_kernel(x_ref, g_ref, sh_ref, sc_ref, mg_ref, w1_ref, w2_ref, fg_ref, o_ref, h_ref, acc_ref,
                *, final_norm):
    j = pl.program_id(2)

    @pl.when(j == 0)
    def _():
        h_ref[...] = _norm_mod(x_ref[...], g_ref[...], sh_ref[...], sc_ref[...]).astype(BF16)
        acc_ref[...] = jnp.zeros_like(acc_ref)

    r = jnp.maximum(jnp.dot(h_ref[...], w1_ref[...], preferred_element_type=F32), 0.0)
    acc_ref[...] += _bdot(r * r, w2_ref[...])

    @pl.when(j == pl.num_programs(2) - 1)
    def _():
        y = x_ref[...] + mg_ref[...] * acc_ref[...]
        if final_norm:
            y = (y * lax.rsqrt(jnp.mean(y * y, axis=-1, keepdims=True) + EPS)) * fg_ref[...]
        o_ref[...] = y


def mlp(x, g, mod3, w1, w2, final_g, ctx, final_norm):
    bsz, L, _ = x.shape
    tl = min(L, 1024)
    tf = 1024
    x_spec = pl.BlockSpec((None, tl, D_MODEL), lambda b, i, j: (b, i, 0))
    vec = pl.BlockSpec((1, D_MODEL), lambda b, i, j: (0, 0))
    return pl.pallas_call(
        functools.partial(_mlp_kernel, final_norm=final_norm),
        grid=(bsz, L // tl, D_FF // tf),
        in_specs=[x_spec, vec, _mod_spec(3, ctx), _mod_spec(4, ctx), _mod_spec(5, ctx),
                  pl.BlockSpec((D_MODEL, tf), lambda b, i, j: (0, j)),
                  pl.BlockSpec((tf, D_MODEL), lambda b, i, j: (j, 0)),
                  vec],
        out_specs=x_spec,
        out_shape=jax.ShapeDtypeStruct((bsz, L, D_MODEL), F32),
        scratch_shapes=[pltpu.VMEM((tl, D_MODEL), BF16), pltpu.VMEM((tl, D_MODEL), F32)],
        compiler_params=_cparams(("parallel", "parallel", "arbitrary"), 48),
        name="mlp",
    )(x, g.reshape(1, D_MODEL), mod3, mod3, mod3, w1, w2, final_g.reshape(1, D_MODEL))


def _sincos_2d(rows, cols, dim):
    quarter = dim // 4
    omega = 1.0 / (10000.0 ** (jnp.arange(quarter, dtype=F32) / quarter))
    er = jnp.arange(rows, dtype=F32)[:, None] * omega[None]
    ec = jnp.arange(cols, dtype=F32)[:, None] * omega[None]
    er = jnp.concatenate([jnp.sin(er), jnp.cos(er)], axis=-1)
    ec = jnp.concatenate([jnp.sin(ec), jnp.cos(ec)], axis=-1)
    emb = jnp.concatenate([
        jnp.broadcast_to(er[:, None, :], (rows, cols, dim // 2)),
        jnp.broadcast_to(ec[None, :, :], (rows, cols, dim // 2))], axis=-1)
    return emb.reshape(rows * cols, dim)


def kernel(x, c, ctx, c_ctx, ada_w, ada_b, norm1_g, norm2_g, w_in, s5_a_re, s5_a_im, s5_log_dt, s5_b_re, s5_b_im, s5_c_re, s5_c_im, s5_d, s5_glu_w, sc_conv_w, sc_out_w, hy_conv_w, hy_conv_b, hy_f_w1, hy_f_b1, hy_f_w2, hy_f_b2, hy_f_w3, hy_f_b3, hy_f_freq, hy_f_w4, hy_d, hy_out_w, out_w, mlp_w1, mlp_w2, final_g):
    bsz, L, _ = x.shape
    x = x + _sincos_2d(L // GRID_W, GRID_W, D_MODEL)[None]
    xc = ctx
    cc = jnp.concatenate([c, c_ctx[None, :], jnp.zeros((MOD_ROWS - bsz - 1, D_MODEL), F32)], axis=0)
    mod_all = ada_mod(cc, ada_w, ada_b)
    for l in range(DEPTH):
        last = l == DEPTH - 1
        mod3 = mod_all[l].reshape(MOD_ROWS, 1, N_MOD * D_MODEL)
        w_in_l = w_in[l].astype(BF16)
        wl = {"s5_glu_w": s5_glu_w[l].astype(BF16), "sc_conv_w": sc_conv_w[l],
              "sc_out_w": sc_out_w[l].astype(BF16), "hy_out_w": hy_out_w[l].astype(BF16),
              "out_w": out_w[l].astype(BF16)}
        lp = {"hy_conv_w": hy_conv_w[l], "hy_conv_b": hy_conv_b[l],
              "hy_f_w1": hy_f_w1[l], "hy_f_b1": hy_f_b1[l], "hy_f_w2": hy_f_w2[l], "hy_f_b2": hy_f_b2[l],
              "hy_f_w3": hy_f_w3[l], "hy_f_b3": hy_f_b3[l], "hy_f_freq": hy_f_freq[l],
              "hy_f_w4": hy_f_w4[l], "hy_d": hy_d[l]}
        w1 = mlp_w1[l].astype(BF16)
        w2 = mlp_w2[l].astype(BF16)
        ops = _s5_operators(s5_a_re[l], s5_a_im[l], s5_log_dt[l], s5_b_re[l], s5_b_im[l],
                            s5_c_re[l], s5_c_im[l], s5_d[l])

        pc = norm_mm(xc, norm1_g[l], mod3, w_in_l[:, :S5_WIDTH] if last else w_in_l, ctx=True)
        yc5, ctx_final = s5_layer(pc[..., OFF_S5:OFF_S5 + S5_WIDTH], ops, None, not last)

        p = norm_mm(x, norm1_g[l], mod3, w_in_l, ctx=False)
        y5, _ = s5_layer(p[..., OFF_S5:OFF_S5 + S5_WIDTH], ops, ctx_final, True)
        x = merge(x, p, y5, *hyena_mix(p, lp), mod3, wl, ctx=False)
        x = mlp(x, norm2_g[l], mod3, w1, w2, final_g, ctx=False, final_norm=last)

        if not last:
            xc = merge(xc, pc, yc5, *hyena_mix(pc, lp), mod3, wl, ctx=True)
            xc = mlp(xc, norm2_g[l], mod3, w1, w2, final_g, ctx=True, final_norm=False)
    return x
```

```python
import functools
import math

import numpy as np
import jax
import jax.numpy as jnp
from jax import lax
from jax.experimental import pallas as pl
from jax.experimental.pallas import tpu as pltpu

F32 = jnp.float32
BF16 = jnp.bfloat16
HIGHEST = lax.Precision.HIGHEST

D_MODEL = 1024
DEPTH = 2
GRID_W = 64
S5_WIDTH = 512
S5_GROUP = 16
S5_GROUPS = S5_WIDTH // S5_GROUP
S5_STATE = 64
SC_WIDTH = 512
HY_WIDTH = 512
HY_BANDS = 16
HY_EMB = 1 + 2 * HY_BANDS
HY_FAST_DECAY = 0.3
HY_SLOW_DECAY = 1.5
HY_DECAY_TARGET = 1e-2
HY_DECAY_SHIFT = 0.05
HY_MAX_DECAY = math.log(HY_DECAY_TARGET) / HY_FAST_DECAY
HY_MIN_DECAY = math.log(HY_DECAY_TARGET) / HY_SLOW_DECAY
N_BRANCH = 3
D_FF = 4 * D_MODEL
N_MOD = 6
EPS = 1e-6
OFF_S5 = 0
OFF_SC = OFF_S5 + S5_WIDTH
OFF_HY = OFF_SC + 3 * SC_WIDTH
OFF_GATE = OFF_HY + 3 * HY_WIDTH
D_IN = OFF_GATE + N_BRANCH * D_MODEL

LANES = 128
SUBLANES = 8
S5_T = 8
S5_BLOCKS = S5_WIDTH // LANES
S5_HALF = 256
EMB_PAD = 40
MOD_ROWS = 8
CTX_ROW = 4
HY_N2 = 256
ROW_TILE = 256


def _cparams(sem, vmem_mb):
    return pltpu.CompilerParams(dimension_semantics=sem, vmem_limit_bytes=vmem_mb * 1024 * 1024)


def _bdot(a, b):
    return jnp.dot(a.astype(BF16), b.astype(BF16), preferred_element_type=F32)


def _hdot(a, b):
    return jnp.dot(a, b, precision=HIGHEST, preferred_element_type=F32)


def _ada_kernel(c_ref, w_ref, b_ref, o_ref):
    c = c_ref[...]
    s = c * jax.nn.sigmoid(c)
    o_ref[...] = _bdot(s, w_ref[...]) + b_ref[...]


def ada_mod(cc, ada_w, ada_b):
    n = N_MOD * D_MODEL
    tn = 1536
    return pl.pallas_call(
        _ada_kernel,
        grid=(DEPTH, n // tn),
        in_specs=[
            pl.BlockSpec((MOD_ROWS, D_MODEL), lambda l, j: (0, 0)),
            pl.BlockSpec((None, D_MODEL, tn), lambda l, j: (l, 0, j)),
            pl.BlockSpec((None, 1, tn), lambda l, j: (l, 0, j)),
        ],
        out_specs=pl.BlockSpec((None, MOD_ROWS, tn), lambda l, j: (l, 0, j)),
        out_shape=jax.ShapeDtypeStruct((DEPTH, MOD_ROWS, n), F32),
        compiler_params=_cparams(("parallel", "parallel"), 40),
        name="ada_mod",
    )(cc, ada_w, ada_b.reshape(DEPTH, 1, n))


def _mod_spec(chunk, ctx):
    if ctx:
        return pl.BlockSpec((None, 1, D_MODEL), lambda b, *_: (CTX_ROW, 0, chunk))
    return pl.BlockSpec((None, 1, D_MODEL), lambda b, *_: (b, 0, chunk))


def _norm_mod(x, g, sh, sc):
    y = x * lax.rsqrt(jnp.mean(x * x, axis=-1, keepdims=True) + EPS)
    return (y * g) * (1.0 + sc) + sh


def _norm_mm_kernel(x_ref, g_ref, sh_ref, sc_ref, w_ref, o_ref, h_ref):
    @pl.when(pl.program_id(2) == 0)
    def _():
        h_ref[...] = _norm_mod(x_ref[...], g_ref[...], sh_ref[...], sc_ref[...]).astype(BF16)

    o_ref[...] = jnp.dot(h_ref[...], w_ref[...], preferred_element_type=F32)


def norm_mm(x, g, mod3, w, ctx):
    bsz, L, _ = x.shape
    n = w.shape[1]
    tl = min(L, 1024)
    tn = n // 4 if n % (4 * LANES) == 0 and n > 2048 else n
    return pl.pallas_call(
        _norm_mm_kernel,
        grid=(bsz, L // tl, n // tn),
        in_specs=[
            pl.BlockSpec((None, tl, D_MODEL), lambda b, i, j: (b, i, 0)),
            pl.BlockSpec((1, D_MODEL), lambda b, i, j: (0, 0)),
            _mod_spec(0, ctx),
            _mod_spec(1, ctx),
            pl.BlockSpec((D_MODEL, tn), lambda b, i, j: (0, j)),
        ],
        out_specs=pl.BlockSpec((None, tl, tn), lambda b, i, j: (b, i, j)),
        out_shape=jax.ShapeDtypeStruct((bsz, L, n), F32),
        scratch_shapes=[pltpu.VMEM((tl, D_MODEL), BF16)],
        compiler_params=_cparams(("parallel", "parallel", "arbitrary"), 48),
        name="norm_mm",
    )(x, g.reshape(1, D_MODEL), mod3, mod3, w)


@functools.lru_cache(maxsize=None)
def _s5_spread_consts():
    T, P = S5_T, S5_GROUP
    src = np.arange(T * P)
    dst = np.arange(T * LANES)
    spread = (src[:, None] // P == dst[None, :] // LANES) & (src[:, None] % P == dst[None, :] % P)
    grp = (dst // P) % 8
    same = grp[:, None] == grp[None, :]
    return spread.astype(np.float32).astype(BF16), same.astype(np.float32).astype(BF16)


def _s5_operators(a_re, a_im, log_dt, b_re, b_im, c_re, c_im, d_skip):
    T = S5_T
    G, N, P = S5_GROUPS, S5_STATE, S5_GROUP
    taus = jnp.arange(T + 1, dtype=F32)[:, None, None]
    ops = []
    for k in range(2):
        ar, ai = a_re[k].astype(F32), a_im[k].astype(F32)
        dt = jnp.exp(log_dt[k].astype(F32))[:, None]
        mag = jnp.exp(ar * dt * taus)
        ang = ai * dt * taus
        pr, pi = mag * jnp.cos(ang), mag * jnp.sin(ang)
        den = ar * ar + ai * ai
        nr, ni = pr[1] - 1.0, pi[1]
        f_re = (nr * ar + ni * ai) / den
        f_im = (ni * ar - nr * ai) / den
        br, bi = b_re[k].astype(F32), b_im[k].astype(F32)
        bbr = f_re[..., None] * br - f_im[..., None] * bi
        bbi = f_re[..., None] * bi + f_im[..., None] * br
        er = pr[..., None] * bbr - pi[..., None] * bbi
        ei = pr[..., None] * bbi + pi[..., None] * bbr
        cr, ci = c_re[k].astype(F32), c_im[k].astype(F32)
        kk = (jnp.einsum("gqn,tgnp->tgqp", cr, er[:T], precision=HIGHEST)
              - jnp.einsum("gqn,tgnp->tgqp", ci, ei[:T], precision=HIGHEST))
        car = cr[None] * pr[:, :, None, :] - ci[None] * pi[:, :, None, :]
        cai = cr[None] * pi[:, :, None, :] + ci[None] * pr[:, :, None, :]
        ops.append((pr, pi, er, ei, kk, car, cai))

    j_in = jnp.arange(T)[:, None]
    j_out = jnp.arange(T)[None, :]
    tau = (j_out - j_in)[..., None, None, None]
    dterm = jnp.eye(P, dtype=F32)[None] * d_skip.astype(F32).reshape(G, 1, P)
    ksum = (jnp.where(tau >= 0, ops[0][4][jnp.clip(j_out - j_in, 0, T - 1)], 0.0)
            + jnp.where(tau <= 0, ops[1][4][jnp.clip(j_in - j_out, 0, T - 1)], 0.0)
            + jnp.where(tau == 0, dterm[None, None], 0.0))
    ks = ksum.reshape(T, T, S5_BLOCKS, 8, P, P).transpose(2, 0, 3, 5, 1, 4)
    ks = ks.reshape(S5_BLOCKS, T * LANES, T * P).astype(BF16)
    spread, same_group = _s5_spread_consts()
    m_op = jnp.einsum("brk,kc->brc", ks, spread, preferred_element_type=BF16) * same_group[None]

    lane = jnp.arange(2 * S5_HALF)
    grp_half = (jnp.arange(8)[None, :, None]
                == jnp.arange(2)[:, None, None] * 4 + ((lane % S5_HALF) // N)[None, None, :]).astype(BF16)
    grp_all = jnp.concatenate([grp_half[0], grp_half[1]], axis=-1)

    def b_mat(er, ei, sel):
        def one(e):
            e6 = e[sel].reshape(T, S5_BLOCKS, 2, 4, N, P)
            return e6.transpose(1, 2, 0, 5, 3, 4).reshape(S5_BLOCKS, 2, T, P, S5_HALF)
        v = jnp.concatenate([one(er), one(ei)], axis=-1).astype(BF16)
        full = v[:, :, :, None, :, :] * grp_half[None, :, None, :, None, :]
        return full.reshape(S5_BLOCKS, 2, T * LANES, 2 * S5_HALF)

    def c_mat_t(car, cai, sel):
        def one(c):
            c6 = c[sel].reshape(T, S5_BLOCKS, 2, 4, P, N)
            return c6.transpose(1, 0, 4, 2, 3, 5).reshape(S5_BLOCKS, T, P, 2, S5_HALF)
        v = jnp.stack([one(car), -one(cai)], axis=4).reshape(S5_BLOCKS, T, P, 4 * S5_HALF).astype(BF16)
        full = v[:, :, None, :, :] * grp_all[None, None, :, None, :]
        return full.reshape(S5_BLOCKS, T * LANES, 4 * S5_HALF)

    jj = jnp.arange(T)
    b_f = b_mat(ops[0][2], ops[0][3], T - 1 - jj)
    b_r = b_mat(ops[1][2], ops[1][3], jj)
    c_f = c_mat_t(ops[0][5], ops[0][6], jj + 1)
    c_r = c_mat_t(ops[1][5], ops[1][6], T - jj)

    def at(pr, pi):
        r = pr[T].reshape(S5_BLOCKS, 2, 1, S5_HALF)
        i = pi[T].reshape(S5_BLOCKS, 2, 1, S5_HALF)
        return jnp.concatenate([r, i], axis=-1)

    return dict(m=m_op, b_f=b_f, b_r=b_r, c_f=c_f, c_r=c_r,
                at_f=at(ops[0][0], ops[0][1]), at_r=at(ops[1][0], ops[1][1]))


def _cmul_add(a, s, d):
    h = S5_HALF
    ar, ai = a[:, :h], a[:, h:]
    sr, si = s[:, :h], s[:, h:]
    return jnp.concatenate([ar * sr - ai * si, ar * si + ai * sr], axis=1) + d


def _s5_scan_kernel(x_ref, bf_ref, br_ref, atf_ref, atr_ref, if_ref, ir_ref,
                    s_ref, r_ref, ff_ref, fr_ref, *, rows):
    rc = min(rows, 256)
    for r0 in range(0, rows, rc):
        xs = x_ref[r0:r0 + rc, :]
        s_ref[r0:r0 + rc, :] = jnp.dot(xs, bf_ref[...], preferred_element_type=F32)
        r_ref[r0:r0 + rc, :] = jnp.dot(xs, br_ref[...], preferred_element_type=F32)

    ntile = rows // SUBLANES
    atf = atf_ref[...]
    atr = atr_ref[...]
    low = lax.broadcasted_iota(jnp.int32, (SUBLANES, 2 * S5_HALF), 0) < 4

    def body(i, carry):
        s, r = carry
        fo = pl.multiple_of(i * SUBLANES, SUBLANES)
        d = s_ref[pl.ds(fo, SUBLANES), :]
        t1 = pltpu.roll(_cmul_add(atf, s, d), 4, 0)
        t2 = _cmul_add(atf, t1, d)
        s_ref[pl.ds(fo, SUBLANES), :] = jnp.where(low, s, t1)
        ro = pl.multiple_of((ntile - 1 - i) * SUBLANES, SUBLANES)
        e = r_ref[pl.ds(ro, SUBLANES), :]
        u1 = pltpu.roll(_cmul_add(atr, r, e), 4, 0)
        u2 = _cmul_add(atr, u1, e)
        r_ref[pl.ds(ro, SUBLANES), :] = jnp.where(low, u1, r)
        return pltpu.roll(t2, 4, 0), pltpu.roll(u2, 4, 0)

    s0 = if_ref[...]
    r0 = pltpu.roll(ir_ref[...], 4, 0)
    s, r = lax.fori_loop(0, ntile, body, (s0, r0))
    ff_ref[...] = s
    fr_ref[...] = pltpu.roll(r, 4, 0)


def s5_scan(x4, ops, init_f, init_r):
    nb, rows, tk = x4.shape
    w = 2 * S5_HALF
    kern = functools.partial(_s5_scan_kernel, rows=rows)
    st_spec = pl.BlockSpec((None, None, SUBLANES, w), lambda b, h: (b, h, 0, 0))
    out_shape = (
        jax.ShapeDtypeStruct((nb, rows, 2 * w), F32),
        jax.ShapeDtypeStruct((nb, rows, 2 * w), F32),
        jax.ShapeDtypeStruct((nb, 2, SUBLANES, w), F32),
        jax.ShapeDtypeStruct((nb, 2, SUBLANES, w), F32),
    )
    return pl.pallas_call(
        kern,
        grid=(nb, 2),
        in_specs=[
            pl.BlockSpec((None, rows, tk), lambda b, h: (b, 0, 0)),
            pl.BlockSpec((None, None, tk, w), lambda b, h: (b, h, 0, 0)),
            pl.BlockSpec((None, None, tk, w), lambda b, h: (b, h, 0, 0)),
            pl.BlockSpec((None, None, 1, w), lambda b, h: (b, h, 0, 0)),
            pl.BlockSpec((None, None, 1, w), lambda b, h: (b, h, 0, 0)),
            st_spec, st_spec,
        ],
        out_specs=(
            pl.BlockSpec((None, rows, w), lambda b, h: (b, 0, h)),
            pl.BlockSpec((None, rows, w), lambda b, h: (b, 0, h)),
            st_spec, st_spec,
        ),
        out_shape=out_shape,
        compiler_params=_cparams(("parallel", "parallel"), 48),
        name="s5_scan",
    )(x4, ops["b_f"], ops["b_r"], ops["at_f"], ops["at_r"], init_f, init_r)


def _s5_out_kernel(x_ref, s_ref, r_ref, m_ref, cf_ref, cr_ref, o_ref):
    nt = (((1,), (1,)), ((), ()))
    acc = jnp.dot(x_ref[...], m_ref[...], preferred_element_type=F32)
    acc += lax.dot_general(s_ref[...].astype(BF16), cf_ref[...], nt, preferred_element_type=F32)
    acc += lax.dot_general(r_ref[...].astype(BF16), cr_ref[...], nt, preferred_element_type=F32)
    o_ref[...] = acc


def s5_out(x4, s_st, r_st, ops):
    nb, rows, tk = x4.shape
    tr = min(rows, 512)
    w = 4 * S5_HALF
    return pl.pallas_call(
        _s5_out_kernel,
        grid=(nb, rows // tr),
        in_specs=[
            pl.BlockSpec((None, tr, tk), lambda b, i: (b, i, 0)),
            pl.BlockSpec((None, tr, w), lambda b, i: (b, i, 0)),
            pl.BlockSpec((None, tr, w), lambda b, i: (b, i, 0)),
            pl.BlockSpec((None, tk, tk), lambda b, i: (b, 0, 0)),
            pl.BlockSpec((None, tk, w), lambda b, i: (b, 0, 0)),
            pl.BlockSpec((None, tk, w), lambda b, i: (b, 0, 0)),
        ],
        out_specs=pl.BlockSpec((None, tr, tk), lambda b, i: (b, i, 0)),
        out_shape=jax.ShapeDtypeStruct((nb, rows, tk), F32),
        compiler_params=_cparams(("parallel", "parallel"), 48),
        name="s5_out",
    )(x4, s_st, r_st, ops["m"], ops["c_f"], ops["c_r"])


def _to_chunk_rows(u):
    bsz, L, _ = u.shape
    nc = L // S5_T
    x = u.astype(BF16).reshape(bsz, nc, S5_T, S5_BLOCKS, LANES)
    return x.transpose(3, 1, 0, 2, 4).reshape(S5_BLOCKS, nc * bsz, S5_T * LANES)


def _from_chunk_rows(y4, bsz):
    nb, rows, _ = y4.shape
    nc = rows // bsz
    y = y4.reshape(nb, nc, bsz, S5_T, LANES).transpose(2, 1, 3, 0, 4)
    return y.reshape(bsz, nc * S5_T, S5_WIDTH)


def s5_layer(u, ops, init, readout):
    bsz = u.shape[0]
    x4 = _to_chunk_rows(u)
    if init is None:
        z = jnp.zeros((S5_BLOCKS, 2, SUBLANES, 2 * S5_HALF), F32)
        init = (z, z)
    s_st, r_st, fin_f, fin_r = s5_scan(x4, ops, init[0], init[1])
    y = _from_chunk_rows(s5_out(x4, s_st, r_st, ops), bsz) if readout else None
    return y, (fin_f, fin_r)


def _conv3(x, prev_row, next_row, w, first, last):
    n = x.shape[0]
    rows = lax.broadcasted_iota(jnp.int32, x.shape, 0)
    pz = jnp.where(first, 0.0, prev_row)
    nz = jnp.where(last, 0.0, next_row)
    xp = jnp.where(rows == 0, pz, pltpu.roll(x, 1, 0))
    xn = jnp.where(rows == n - 1, nz, pltpu.roll(x, n - 1, 0))
    return w[0:1, :] * xp + w[1:2, :] * x + w[2:3, :] * xn


def _halo_specs(tl, L, width, col):
    tb = tl // SUBLANES
    nb = L // SUBLANES
    cur = pl.BlockSpec((None, tl, width), lambda b, i: (b, i, col))
    prv = pl.BlockSpec((None, SUBLANES, width), lambda b, i: (b, jnp.maximum(i * tb - 1, 0), col))
    nxt = pl.BlockSpec((None, SUBLANES, width), lambda b, i: (b, jnp.minimum((i + 1) * tb, nb - 1), col))
    return [cur, prv, nxt]


def _dft_plan(L):
    n = 2 * L
    n1 = n // HY_N2
    return n, n1, n1 // 2, n1 // 2 + 1


def _split_bf16(a):
    hi = np.asarray(a, np.float32).astype(BF16)
    lo = (np.asarray(a, np.float32) - hi.astype(np.float32)).astype(BF16)
    return hi, lo


@functools.lru_cache(maxsize=None)
def _dft_consts(L):
    n, n1, h, nk = _dft_plan(L)
    k1 = np.arange(nk)[:, None]
    m1 = np.arange(h)[None, :]
    th = 2.0 * np.pi * k1 * m1 / n1
    wt = np.where((k1 == 0) | (k1 == h), 1.0, 2.0) / n
    k2 = np.arange(HY_N2)[:, None]
    m2 = np.arange(HY_N2)[None, :]
    ph = -2.0 * np.pi * (k2 * m2 / HY_N2)[None] - 2.0 * np.pi * np.arange(nk)[:, None, None] * m2[None] / n
    gr, gi = np.cos(ph), np.sin(ph)
    wf = np.concatenate([gr.transpose(0, 2, 1), gi.transpose(0, 2, 1)], axis=2)
    wi = np.concatenate([gr, -gi], axis=2)
    f = lambda a: np.asarray(a, np.float32)
    wf_hi, wf_lo = _split_bf16(wf)
    wi_hi, wi_lo = _split_bf16(wi)
    return dict(c1=f(np.cos(th)), s1=f(-np.sin(th)),
                ic=f((wt * np.cos(th)).T), isn=f((wt * np.sin(th)).T),
                wf_hi=wf_hi, wf_lo=wf_lo, wi_hi=wi_hi, wi_lo=wi_lo)


HY_RC = 32
W = HY_WIDTH


def _row_chunks(body):
    def step(ci, carry):
        body(pl.multiple_of(ci * HY_RC, HY_RC))
        return carry
    lax.fori_loop(0, W // HY_RC, step, 0)


def _store_split(hi_ref, lo_ref, r0, vr, vi):
    for off, v in ((0, vr), (W, vi)):
        hi = v.astype(BF16)
        hi_ref[pl.ds(off + r0, HY_RC), :] = hi
        lo_ref[pl.ds(off + r0, HY_RC), :] = (v - hi.astype(F32)).astype(BF16)


def _stage1(x_ref, c_ref, s_ref, k, h, hi_ref, lo_ref):
    def body(r0):
        x0 = x_ref[0, pl.ds(r0, HY_RC), :]
        ar = c_ref[k, 0] * x0
        ai = s_ref[k, 0] * x0
        for m in range(1, h):
            xm = x_ref[m, pl.ds(r0, HY_RC), :]
            ar += c_ref[k, m] * xm
            ai += s_ref[k, m] * xm
        _store_split(hi_ref, lo_ref, r0, ar, ai)
    _row_chunks(body)


def _cmatmul(hi_ref, lo_ref, w_hi, w_lo, r_ref):
    a_hi = hi_ref[...]
    r_ref[...] = (jnp.dot(a_hi, w_hi, preferred_element_type=F32)
                  + jnp.dot(lo_ref[...], w_hi, preferred_element_type=F32)
                  + jnp.dot(a_hi, w_lo, preferred_element_type=F32))


def _cresult(r_ref, r0):
    top = r_ref[pl.ds(r0, HY_RC), :]
    bot = r_ref[pl.ds(W + r0, HY_RC), :]
    return top[:, :HY_N2] - bot[:, HY_N2:], top[:, HY_N2:] + bot[:, :HY_N2]


def _hy_pre_kernel(v_ref, vp_ref, vn_ref, a_ref, ap_ref, an_ref, z_ref, zp_ref, zn_ref,
                   w_ref, b_ref, mt_ref, x0_ref):
    i = pl.program_id(1)
    first = i == 0
    last = i == pl.num_programs(1) - 1
    w = w_ref[...]
    b = b_ref[...]

    def part(c, p, n, k):
        sl = slice(k * HY_WIDTH, (k + 1) * HY_WIDTH)
        return _conv3(c[...], p[SUBLANES - 1:SUBLANES, :], n[0:1, :], w[:, sl], first, last) + b[:, sl]

    v = part(v_ref, vp_ref, vn_ref, 0)
    x1 = part(a_ref, ap_ref, an_ref, 1)
    mt_ref[...] = (x1 * v).T
    x0_ref[...] = part(z_ref, zp_ref, zn_ref, 2)


def hy_pre(p, conv_w, conv_b):
    bsz, L, _ = p.shape
    tl = ROW_TILE
    c0 = OFF_HY // HY_WIDTH
    specs = []
    for k in range(3):
        specs += _halo_specs(tl, L, HY_WIDTH, c0 + k)
    specs += [pl.BlockSpec((3, 3 * HY_WIDTH), lambda b, i: (0, 0)),
              pl.BlockSpec((1, 3 * HY_WIDTH), lambda b, i: (0, 0))]
    return pl.pallas_call(
        _hy_pre_kernel,
        grid=(bsz, L // tl),
        in_specs=specs,
        out_specs=(pl.BlockSpec((None, None, HY_WIDTH, HY_N2), lambda b, i: (b, i, 0, 0)),
                   pl.BlockSpec((None, tl, HY_WIDTH), lambda b, i: (b, i, 0))),
        out_shape=(jax.ShapeDtypeStruct((bsz, L // HY_N2, HY_WIDTH, HY_N2), F32),
                   jax.ShapeDtypeStruct((bsz, L, HY_WIDTH), F32)),
        compiler_params=_cparams(("parallel", "parallel"), 32),
        name="hy_pre",
    )(*([p] * 9), conv_w, conv_b.reshape(1, -1))


def _hy_filter_kernel(z_ref, w1_ref, b1_ref, w2_ref, b2_ref, w3_ref, b3_ref, fr_ref, w4_ref,
                      dl_ref, hf_ref, hb_ref, nrm_ref, *, tl):
    i = pl.program_id(0)
    z = z_ref[...]
    fr = fr_ref[...]
    h = jnp.sin(fr * (_hdot(z, w1_ref[...]) + b1_ref[...]))
    h = jnp.sin(fr * (_hdot(h, w2_ref[...]) + b2_ref[...]))
    h = jnp.sin(fr * (_hdot(h, w3_ref[...]) + b3_ref[...]))
    h = _hdot(h, w4_ref[...])
    t = z[:, 0:1]
    window = jnp.exp(-t * dl_ref[...]) + HY_DECAY_SHIFT
    hf = h[:, :HY_WIDTH] * window
    rows = lax.broadcasted_iota(jnp.int32, (tl, HY_WIDTH), 0) + i * tl
    hb = jnp.where(rows == 0, 0.0, h[:, HY_WIDTH:] * window)
    hf_ref[...] = hf.T
    hb_ref[...] = hb.T
    part = jnp.sum(jnp.abs(hf) + jnp.abs(hb), axis=0, keepdims=True)

    @pl.when(i == 0)
    def _():
        nrm_ref[...] = jnp.zeros_like(nrm_ref)

    nrm_ref[...] += part


def hy_filter_taps(L, lp):
    t = jnp.linspace(0.0, 1.0, L, dtype=F32)[:, None]
    ang = 2.0 * math.pi * jnp.arange(L, dtype=F32)[:, None] / L
    bands = jnp.linspace(1e-4, HY_BANDS - 1, HY_BANDS, dtype=F32)[None, :]
    z = jnp.concatenate([t, jnp.cos(bands * ang), -jnp.sin(bands * ang),
                         jnp.zeros((L, EMB_PAD - HY_EMB), F32)], axis=-1)
    w1 = jnp.concatenate([lp["hy_f_w1"], jnp.zeros((EMB_PAD - HY_EMB, lp["hy_f_w1"].shape[1]), F32)], axis=0)
    deltas = jnp.abs(jnp.linspace(HY_MIN_DECAY, HY_MAX_DECAY, HY_WIDTH, dtype=F32))[None, :]
    tl = HY_N2
    hid = w1.shape[1]
    full = lambda shape: pl.BlockSpec(shape, lambda i: (0,) * len(shape))
    o_spec = pl.BlockSpec((None, HY_WIDTH, HY_N2), lambda i: (i, 0, 0))
    shp = jax.ShapeDtypeStruct((L // HY_N2, HY_WIDTH, HY_N2), F32)
    return pl.pallas_call(
        functools.partial(_hy_filter_kernel, tl=tl),
        grid=(L // tl,),
        in_specs=[pl.BlockSpec((tl, EMB_PAD), lambda i: (i, 0)),
                  full((EMB_PAD, hid)), full((1, hid)), full((hid, hid)), full((1, hid)),
                  full((hid, hid)), full((1, hid)), full((1, hid)), full((hid, 2 * HY_WIDTH)),
                  full((1, HY_WIDTH))],
        out_specs=(o_spec, o_spec, pl.BlockSpec((1, HY_WIDTH), lambda i: (0, 0))),
        out_shape=(shp, shp, jax.ShapeDtypeStruct((1, HY_WIDTH), F32)),
        compiler_params=_cparams(("arbitrary",), 32),
        name="hy_filter_taps",
    )(z, w1, lp["hy_f_b1"].reshape(1, -1), lp["hy_f_w2"], lp["hy_f_b2"].reshape(1, -1),
      lp["hy_f_w3"], lp["hy_f_b3"].reshape(1, -1), lp["hy_f_freq"].reshape(1, -1), lp["hy_f_w4"], deltas)


def _hy_spectrum_kernel(c_ref, s_ref, hf_ref, hb_ref, whi_ref, wlo_ref, inv_ref, k_ref,
                        hi_ref, lo_ref, rf_ref, rb_ref, *, h):
    k = pl.program_id(0)
    _stage1(hf_ref, c_ref, s_ref, k, h, hi_ref, lo_ref)
    _cmatmul(hi_ref, lo_ref, whi_ref[...], wlo_ref[...], rf_ref)
    _stage1(hb_ref, c_ref, s_ref, k, h, hi_ref, lo_ref)
    _cmatmul(hi_ref, lo_ref, whi_ref[...], wlo_ref[...], rb_ref)

    def body(r0):
        fr, fi = _cresult(rf_ref, r0)
        br, bi = _cresult(rb_ref, r0)
        inv = inv_ref[pl.ds(r0, HY_RC), :]
        k_ref[pl.ds(r0, HY_RC), :] = jnp.concatenate([(fr + br) * inv, (fi - bi) * inv], axis=1)
    _row_chunks(body)


def hy_spectrum(hf_t, hb_t, inv_norm, L):
    n, n1, h, nk = _dft_plan(L)
    cs = _dft_consts(L)
    smem = pl.BlockSpec(memory_space=pltpu.SMEM)
    t_spec = pl.BlockSpec((h, HY_WIDTH, HY_N2), lambda k: (0, 0, 0))
    w_spec = pl.BlockSpec((None, HY_N2, 2 * HY_N2), lambda k: (k, 0, 0))
    return pl.pallas_call(
        functools.partial(_hy_spectrum_kernel, h=h),
        grid=(nk,),
        in_specs=[smem, smem, t_spec, t_spec, w_spec, w_spec,
                  pl.BlockSpec((HY_WIDTH, 1), lambda k: (0, 0))],
        out_specs=pl.BlockSpec((None, HY_WIDTH, 2 * HY_N2), lambda k: (k, 0, 0)),
        out_shape=jax.ShapeDtypeStruct((nk, HY_WIDTH, 2 * HY_N2), F32),
        scratch_shapes=_hy_scratch(2),
        compiler_params=_cparams(("parallel",), 48),
        name="hy_spectrum",
    )(cs["c1"], cs["s1"], hf_t, hb_t, cs["wf_hi"], cs["wf_lo"], inv_norm)


def _hy_scratch(n_results):
    return ([pltpu.VMEM((2 * W, HY_N2), BF16)] * 2
            + [pltpu.VMEM((2 * W, 2 * HY_N2), F32)] * n_results)


def _hy_conv_kernel(c_ref, s_ref, ic_ref, is_ref, m_ref, kf_ref, fhi_ref, flo_ref, ihi_ref, ilo_ref,
                    d_ref, o_ref, hi_ref, lo_ref, r_ref, *, h):
    k = pl.program_id(1)
    _stage1(m_ref, c_ref, s_ref, k, h, hi_ref, lo_ref)
    _cmatmul(hi_ref, lo_ref, fhi_ref[...], flo_ref[...], r_ref)

    def spectrum_product(r0):
        yr, yi = _cresult(r_ref, r0)
        kf = kf_ref[pl.ds(r0, HY_RC), :]
        kr, ki = kf[:, :HY_N2], kf[:, HY_N2:]
        _store_split(hi_ref, lo_ref, r0, yr * kr - yi * ki, yr * ki + yi * kr)
    _row_chunks(spectrum_product)
    _cmatmul(hi_ref, lo_ref, ihi_ref[...], ilo_ref[...], r_ref)

    @pl.when(k == 0)
    def _():
        def body(r0):
            br, bi = _cresult(r_ref, r0)
            d = d_ref[pl.ds(r0, HY_RC), :]
            for m in range(h):
                o_ref[m, pl.ds(r0, HY_RC), :] = (m_ref[m, pl.ds(r0, HY_RC), :] * d
                                                 + (ic_ref[m, 0] * br - is_ref[m, 0] * bi))
        _row_chunks(body)

    @pl.when(k > 0)
    def _():
        def body(r0):
            br, bi = _cresult(r_ref, r0)
            for m in range(h):
                o_ref[m, pl.ds(r0, HY_RC), :] += ic_ref[m, k] * br - is_ref[m, k] * bi
        _row_chunks(body)


def hy_conv(m_t, kf, d_col, L):
    n, n1, h, nk = _dft_plan(L)
    cs = _dft_consts(L)
    bsz = m_t.shape[0]
    smem = pl.BlockSpec(memory_space=pltpu.SMEM)
    t_spec = pl.BlockSpec((None, h, HY_WIDTH, HY_N2), lambda b, k: (b, 0, 0, 0))
    w_spec = pl.BlockSpec((None, HY_N2, 2 * HY_N2), lambda b, k: (k, 0, 0))
    return pl.pallas_call(
        functools.partial(_hy_conv_kernel, h=h),
        grid=(bsz, nk),
        in_specs=[smem, smem, smem, smem, t_spec,
                  pl.BlockSpec((None, HY_WIDTH, 2 * HY_N2), lambda b, k: (k, 0, 0)),
                  w_spec, w_spec, w_spec, w_spec,
                  pl.BlockSpec((HY_WIDTH, 1), lambda b, k: (0, 0))],
        out_specs=t_spec,
        out_shape=jax.ShapeDtypeStruct(m_t.shape, F32),
        scratch_shapes=_hy_scratch(1),
        compiler_params=_cparams(("parallel", "arbitrary"), 52),
        name="hy_conv",
    )(cs["c1"], cs["s1"], cs["ic"], cs["isn"], m_t, kf,
      cs["wf_hi"], cs["wf_lo"], cs["wi_hi"], cs["wi_lo"], d_col)


def hyena_mix(p, lp):
    L = p.shape[1]
    hf_t, hb_t, nrm = hy_filter_taps(L, lp)
    kf = hy_spectrum(hf_t, hb_t, (1.0 / nrm).reshape(HY_WIDTH, 1), L)
    m_t, x0 = hy_pre(p, lp["hy_conv_w"], lp["hy_conv_b"])
    z_t = hy_conv(m_t, kf, lp["hy_d"].astype(F32).reshape(HY_WIDTH, 1), L)
    return z_t, x0


def _gelu_tanh(x):
    return 0.5 * x * (1.0 + jnp.tanh(math.sqrt(2.0 / math.pi) * (x + 0.044715 * (x * x * x))))


def _merge_kernel(x_ref, y5_ref, zt_ref, x0_ref,
                  sx_ref, sxp_ref, sxn_ref, sb_ref, sc_ref, scp_ref, scn_ref,
                  g0a_ref, g0b_ref, g1a_ref, g1b_ref, g2a_ref, g2b_ref, mg_ref,
                  glu_ref, scw_ref, sco_ref, hyo_ref, ow_ref, o_ref):
    i = pl.program_id(1)
    first = i == 0
    last = i == pl.num_programs(1) - 1
    ag = _bdot(_gelu_tanh(y5_ref[...]), glu_ref[...])
    y_s5 = ag[:, :D_MODEL] * jax.nn.sigmoid(ag[:, D_MODEL:])
    l7 = slice(SUBLANES - 1, SUBLANES)
    cx = sc_ref[...] * sx_ref[...]
    cxp = scp_ref[l7, :] * sxp_ref[l7, :]
    cxn = scn_ref[0:1, :] * sxn_ref[0:1, :]
    y_sc = _bdot(sb_ref[...] * _conv3(cx, cxp, cxn, scw_ref[...], first, last), sco_ref[...])
    y_hy = _bdot(x0_ref[...] * zt_ref[...].T, hyo_ref[...])
    gate = lambda a, b: jax.nn.sigmoid(jnp.concatenate([a[...], b[...]], axis=1))
    m = gate(g0a_ref, g0b_ref) * y_s5 + gate(g1a_ref, g1b_ref) * y_sc + gate(g2a_ref, g2b_ref) * y_hy
    o_ref[...] = x_ref[...] + mg_ref[...] * _bdot(m, ow_ref[...])


def merge(x, p, y5, z_t, x0, mod3, wl, ctx):
    bsz, L, _ = x.shape
    tl = ROW_TILE
    row = lambda width, col: pl.BlockSpec((None, tl, width), lambda b, i: (b, i, col))
    full = lambda shape: pl.BlockSpec(shape, lambda b, i: (0,) * len(shape))
    c_sc = OFF_SC // SC_WIDTH
    half = D_MODEL // 2
    c_g = OFF_GATE // half
    specs = [row(D_MODEL, 0), row(S5_WIDTH, 0),
             pl.BlockSpec((None, None, HY_WIDTH, HY_N2), lambda b, i: (b, i, 0, 0)), row(HY_WIDTH, 0)]
    specs += _halo_specs(tl, L, SC_WIDTH, c_sc)
    specs += [row(SC_WIDTH, c_sc + 1)]
    specs += _halo_specs(tl, L, SC_WIDTH, c_sc + 2)
    specs += [row(half, c_g + k) for k in range(2 * N_BRANCH)] + [_mod_spec(2, ctx)]
    specs += [full((S5_WIDTH, 2 * D_MODEL)), full((3, SC_WIDTH)), full((SC_WIDTH, D_MODEL)),
              full((HY_WIDTH, D_MODEL)), full((D_MODEL, D_MODEL))]
    return pl.pallas_call(
        _merge_kernel,
        grid=(bsz, L // tl),
        in_specs=specs,
        out_specs=row(D_MODEL, 0),
        out_shape=jax.ShapeDtypeStruct((bsz, L, D_MODEL), F32),
        compiler_params=_cparams(("parallel", "parallel"), 48),
        name="merge",
    )(x, y5, z_t, x0, *([p] * 13), mod3,
      wl["s5_glu_w"], wl["sc_conv_w"], wl["sc_out_w"], wl["hy_out_w"], wl["out_w"])


def _mlp_kernel(x_ref, g_ref, sh_ref, sc_ref, mg_ref, w1_ref, w2_ref, fg_ref, o_ref, h_ref, acc_ref,
                *, final_norm):
    j = pl.program_id(2)

    @pl.when(j == 0)
    def _():
        h_ref[...] = _norm_mod(x_ref[...], g_ref[...], sh_ref[...], sc_ref[...]).astype(BF16)
        acc_ref[...] = jnp.zeros_like(acc_ref)

    r = jnp.maximum(jnp.dot(h_ref[...], w1_ref[...], preferred_element_type=F32), 0.0)
    acc_ref[...] += _bdot(r * r, w2_ref[...])

    @pl.when(j == pl.num_programs(2) - 1)
    def _():
        y = x_ref[...] + mg_ref[...] * acc_ref[...]
        if final_norm:
            y = (y * lax.rsqrt(jnp.mean(y * y, axis=-1, keepdims=True) + EPS)) * fg_ref[...]
        o_ref[...] = y


def mlp(x, g, mod3, w1, w2, final_g, ctx, final_norm):
    bsz, L, _ = x.shape
    tl = min(L, 1024)
    tf = 1024
    x_spec = pl.BlockSpec((None, tl, D_MODEL), lambda b, i, j: (b, i, 0))
    vec = pl.BlockSpec((1, D_MODEL), lambda b, i, j: (0, 0))
    return pl.pallas_call(
        functools.partial(_mlp_kernel, final_norm=final_norm),
        grid=(bsz, L // tl, D_FF // tf),
        in_specs=[x_spec, vec, _mod_spec(3, ctx), _mod_spec(4, ctx), _mod_spec(5, ctx),
                  pl.BlockSpec((D_MODEL, tf), lambda b, i, j: (0, j)),
                  pl.BlockSpec((tf, D_MODEL), lambda b, i, j: (j, 0)),
                  vec],
        out_specs=x_spec,
        out_shape=jax.ShapeDtypeStruct((bsz, L, D_MODEL), F32),
        scratch_shapes=[pltpu.VMEM((tl, D_MODEL), BF16), pltpu.VMEM((tl, D_MODEL), F32)],
        compiler_params=_cparams(("parallel", "parallel", "arbitrary"), 48),
        name="mlp",
    )(x, g.reshape(1, D_MODEL), mod3, mod3, mod3, w1, w2, final_g.reshape(1, D_MODEL))


def _sincos_2d(rows, cols, dim):
    quarter = dim // 4
    omega = 1.0 / (10000.0 ** (jnp.arange(quarter, dtype=F32) / quarter))
    er = jnp.arange(rows, dtype=F32)[:, None] * omega[None]
    ec = jnp.arange(cols, dtype=F32)[:, None] * omega[None]
    er = jnp.concatenate([jnp.sin(er), jnp.cos(er)], axis=-1)
    ec = jnp.concatenate([jnp.sin(ec), jnp.cos(ec)], axis=-1)
    emb = jnp.concatenate([
        jnp.broadcast_to(er[:, None, :], (rows, cols, dim // 2)),
        jnp.broadcast_to(ec[None, :, :], (rows, cols, dim // 2))], axis=-1)
    return emb.reshape(rows * cols, dim)


def kernel(x, c, ctx, c_ctx, ada_w, ada_b, norm1_g, norm2_g, w_in, s5_a_re, s5_a_im, s5_log_dt, s5_b_re, s5_b_im, s5_c_re, s5_c_im, s5_d, s5_glu_w, sc_conv_w, sc_out_w, hy_conv_w, hy_conv_b, hy_f_w1, hy_f_b1, hy_f_w2, hy_f_b2, hy_f_w3, hy_f_b3, hy_f_freq, hy_f_w4, hy_d, hy_out_w, out_w, mlp_w1, mlp_w2, final_g):
    bsz, L, _ = x.shape
    x = x + _sincos_2d(L // GRID_W, GRID_W, D_MODEL)[None]
    xc = ctx
    cc = jnp.concatenate([c, c_ctx[None, :], jnp.zeros((MOD_ROWS - bsz - 1, D_MODEL), F32)], axis=0)
    mod_all = ada_mod(cc, ada_w, ada_b)
    for l in range(DEPTH):
        last = l == DEPTH - 1
        mod3 = mod_all[l].reshape(MOD_ROWS, 1, N_MOD * D_MODEL)
        w_in_l = w_in[l].astype(BF16)
        wl = {"s5_glu_w": s5_glu_w[l].astype(BF16), "sc_conv_w": sc_conv_w[l],
              "sc_out_w": sc_out_w[l].astype(BF16), "hy_out_w": hy_out_w[l].astype(BF16),
              "out_w": out_w[l].astype(BF16)}
        lp = {"hy_conv_w": hy_conv_w[l], "hy_conv_b": hy_conv_b[l],
              "hy_f_w1": hy_f_w1[l], "hy_f_b1": hy_f_b1[l], "hy_f_w2": hy_f_w2[l], "hy_f_b2": hy_f_b2[l],
              "hy_f_w3": hy_f_w3[l], "hy_f_b3": hy_f_b3[l], "hy_f_freq": hy_f_freq[l],
              "hy_f_w4": hy_f_w4[l], "hy_d": hy_d[l]}
        w1 = mlp_w1[l].astype(BF16)
        w2 = mlp_w2[l].astype(BF16)
        ops = _s5_operators(s5_a_re[l], s5_a_im[l], s5_log_dt[l], s5_b_re[l], s5_b_im[l],
                            s5_c_re[l], s5_c_im[l], s5_d[l])

        pc = norm_mm(xc, norm1_g[l], mod3, w_in_l[:, :S5_WIDTH] if last else w_in_l, ctx=True)
        yc5, ctx_final = s5_layer(pc[..., OFF_S5:OFF_S5 + S5_WIDTH], ops, None, not last)

        p = norm_mm(x, norm1_g[l], mod3, w_in_l, ctx=False)
        y5, _ = s5_layer(p[..., OFF_S5:OFF_S5 + S5_WIDTH], ops, ctx_final, True)
        x = merge(x, p, y5, *hyena_mix(p, lp), mod3, wl, ctx=False)
        x = mlp(x, norm2_g[l], mod3, w1, w2, final_g, ctx=False, final_norm=last)

        if not last:
            xc = merge(xc, pc, yc5, *hyena_mix(pc, lp), mod3, wl, ctx=True)
            xc = mlp(xc, norm2_g[l], mod3, w1, w2, final_g, ctx=True, final_norm=False)
    return x
```

```python
import functools
import math

import numpy as np
import jax
import jax.numpy as jnp
from jax import lax
from jax.experimental import pallas as pl
from jax.experimental.pallas import tpu as pltpu

F32 = jnp.float32
BF16 = jnp.bfloat16
HIGHEST = lax.Precision.HIGHEST

D_MODEL = 1024
DEPTH = 2
GRID_W = 64
S5_WIDTH = 512
S5_GROUP = 16
S5_GROUPS = S5_WIDTH // S5_GROUP
S5_STATE = 64
SC_WIDTH = 512
HY_WIDTH = 512
HY_BANDS = 16
HY_EMB = 1 + 2 * HY_BANDS
HY_FAST_DECAY = 0.3
HY_SLOW_DECAY = 1.5
HY_DECAY_TARGET = 1e-2
HY_DECAY_SHIFT = 0.05
HY_MAX_DECAY = math.log(HY_DECAY_TARGET) / HY_FAST_DECAY
HY_MIN_DECAY = math.log(HY_DECAY_TARGET) / HY_SLOW_DECAY
N_BRANCH = 3
D_FF = 4 * D_MODEL
N_MOD = 6
EPS = 1e-6
OFF_S5 = 0
OFF_SC = OFF_S5 + S5_WIDTH
OFF_HY = OFF_SC + 3 * SC_WIDTH
OFF_GATE = OFF_HY + 3 * HY_WIDTH
D_IN = OFF_GATE + N_BRANCH * D_MODEL

LANES = 128
SUBLANES = 8
S5_T = 8
S5_BLOCKS = S5_WIDTH // LANES
S5_HALF = 256
EMB_PAD = 40
MOD_ROWS = 8
CTX_ROW = 4
HY_N2 = 256
ROW_TILE = 256


def _cparams(sem, vmem_mb):
    return pltpu.CompilerParams(dimension_semantics=sem, vmem_limit_bytes=vmem_mb * 1024 * 1024)


def _bdot(a, b):
    return jnp.dot(a.astype(BF16), b.astype(BF16), preferred_element_type=F32)


def _hdot(a, b):
    return jnp.dot(a, b, precision=HIGHEST, preferred_element_type=F32)


def _ada_kernel(c_ref, w_ref, b_ref, o_ref):
    c = c_ref[...]
    s = c * jax.nn.sigmoid(c)
    o_ref[...] = _bdot(s, w_ref[...]) + b_ref[...]


def ada_mod(cc, ada_w, ada_b):
    n = N_MOD * D_MODEL
    tn = 1536
    return pl.pallas_call(
        _ada_kernel,
        grid=(DEPTH, n // tn),
        in_specs=[
            pl.BlockSpec((MOD_ROWS, D_MODEL), lambda l, j: (0, 0)),
            pl.BlockSpec((None, D_MODEL, tn), lambda l, j: (l, 0, j)),
            pl.BlockSpec((None, 1, tn), lambda l, j: (l, 0, j)),
        ],
        out_specs=pl.BlockSpec((None, MOD_ROWS, tn), lambda l, j: (l, 0, j)),
        out_shape=jax.ShapeDtypeStruct((DEPTH, MOD_ROWS, n), F32),
        compiler_params=_cparams(("parallel", "parallel"), 40),
        name="ada_mod",
    )(cc, ada_w, ada_b.reshape(DEPTH, 1, n))


def _mod_spec(chunk, ctx):
    if ctx:
        return pl.BlockSpec((None, 1, D_MODEL), lambda b, *_: (CTX_ROW, 0, chunk))
    return pl.BlockSpec((None, 1, D_MODEL), lambda b, *_: (b, 0, chunk))


def _norm_mod(x, g, sh, sc):
    y = x * lax.rsqrt(jnp.mean(x * x, axis=-1, keepdims=True) + EPS)
    return (y * g) * (1.0 + sc) + sh


def _norm_mm_kernel(x_ref, g_ref, sh_ref, sc_ref, w_ref, o_ref, h_ref):
    @pl.when(pl.program_id(2) == 0)
    def _():
        h_ref[...] = _norm_mod(x_ref[...], g_ref[...], sh_ref[...], sc_ref[...]).astype(BF16)

    o_ref[...] = jnp.dot(h_ref[...], w_ref[...], preferred_element_type=F32)


def norm_mm(x, g, mod3, w, ctx):
    bsz, L, _ = x.shape
    n = w.shape[1]
    tl = min(L, 1024)
    tn = n // 4 if n % (4 * LANES) == 0 and n > 2048 else n
    return pl.pallas_call(
        _norm_mm_kernel,
        grid=(bsz, L // tl, n // tn),
        in_specs=[
            pl.BlockSpec((None, tl, D_MODEL), lambda b, i, j: (b, i, 0)),
            pl.BlockSpec((1, D_MODEL), lambda b, i, j: (0, 0)),
            _mod_spec(0, ctx),
            _mod_spec(1, ctx),
            pl.BlockSpec((D_MODEL, tn), lambda b, i, j: (0, j)),
        ],
        out_specs=pl.BlockSpec((None, tl, tn), lambda b, i, j: (b, i, j)),
        out_shape=jax.ShapeDtypeStruct((bsz, L, n), F32),
        scratch_shapes=[pltpu.VMEM((tl, D_MODEL), BF16)],
        compiler_params=_cparams(("parallel", "parallel", "arbitrary"), 48),
        name="norm_mm",
    )(x, g.reshape(1, D_MODEL), mod3, mod3, w)


@functools.lru_cache(maxsize=None)
def _s5_spread_consts():
    T, P = S5_T, S5_GROUP
    src = np.arange(T * P)
    dst = np.arange(T * LANES)
    spread = (src[:, None] // P == dst[None, :] // LANES) & (src[:, None] % P == dst[None, :] % P)
    grp = (dst // P) % 8
    same = grp[:, None] == grp[None, :]
    return spread.astype(np.float32).astype(BF16), same.astype(np.float32).astype(BF16)


def _s5_operators(a_re, a_im, log_dt, b_re, b_im, c_re, c_im, d_skip):
    T = S5_T
    G, N, P = S5_GROUPS, S5_STATE, S5_GROUP
    taus = jnp.arange(T + 1, dtype=F32)[:, None, None]
    ops = []
    for k in range(2):
        ar, ai = a_re[k].astype(F32), a_im[k].astype(F32)
        dt = jnp.exp(log_dt[k].astype(F32))[:, None]
        mag = jnp.exp(ar * dt * taus)
        ang = ai * dt * taus
        pr, pi = mag * jnp.cos(ang), mag * jnp.sin(ang)
        den = ar * ar + ai * ai
        nr, ni = pr[1] - 1.0, pi[1]
        f_re = (nr * ar + ni * ai) / den
        f_im = (ni * ar - nr * ai) / den
        br, bi = b_re[k].astype(F32), b_im[k].astype(F32)
        bbr = f_re[..., None] * br - f_im[..., None] * bi
        bbi = f_re[..., None] * bi + f_im[..., None] * br
        er = pr[..., None] * bbr - pi[..., None] * bbi
        ei = pr[..., None] * bbi + pi[..., None] * bbr
        cr, ci = c_re[k].astype(F32), c_im[k].astype(F32)
        kk = (jnp.einsum("gqn,tgnp->tgqp", cr, er[:T], precision=HIGHEST)
              - jnp.einsum("gqn,tgnp->tgqp", ci, ei[:T], precision=HIGHEST))
        car = cr[None] * pr[:, :, None, :] - ci[None] * pi[:, :, None, :]
        cai = cr[None] * pi[:, :, None, :] + ci[None] * pr[:, :, None, :]
        ops.append((pr, pi, er, ei, kk, car, cai))

    j_in = jnp.arange(T)[:, None]
    j_out = jnp.arange(T)[None, :]
    tau = (j_out - j_in)[..., None, None, None]
    dterm = jnp.eye(P, dtype=F32)[None] * d_skip.astype(F32).reshape(G, 1, P)
    ksum = (jnp.where(tau >= 0, ops[0][4][jnp.clip(j_out - j_in, 0, T - 1)], 0.0)
            + jnp.where(tau <= 0, ops[1][4][jnp.clip(j_in - j_out, 0, T - 1)], 0.0)
            + jnp.where(tau == 0, dterm[None, None], 0.0))
    ks = ksum.reshape(T, T, S5_BLOCKS, 8, P, P).transpose(2, 0, 3, 5, 1, 4)
    ks = ks.reshape(S5_BLOCKS, T * LANES, T * P).astype(BF16)
    spread, same_group = _s5_spread_consts()
    m_op = jnp.einsum("brk,kc->brc", ks, spread, preferred_element_type=BF16) * same_group[None]

    lane = jnp.arange(2 * S5_HALF)
    grp_half = (jnp.arange(8)[None, :, None]
                == jnp.arange(2)[:, None, None] * 4 + ((lane % S5_HALF) // N)[None, None, :]).astype(BF16)
    grp_all = jnp.concatenate([grp_half[0], grp_half[1]], axis=-1)

    def b_mat(er, ei, sel):
        def one(e):
            e6 = e[sel].reshape(T, S5_BLOCKS, 2, 4, N, P)
            return e6.transpose(1, 2, 0, 5, 3, 4).reshape(S5_BLOCKS, 2, T, P, S5_HALF)
        v = jnp.concatenate([one(er), one(ei)], axis=-1).astype(BF16)
        full = v[:, :, :, None, :, :] * grp_half[None, :, None, :, None, :]
        return full.reshape(S5_BLOCKS, 2, T * LANES, 2 * S5_HALF)

    def c_mat_t(car, cai, sel):
        def one(c):
            c6 = c[sel].reshape(T, S5_BLOCKS, 2, 4, P, N)
            return c6.transpose(1, 0, 4, 2, 3, 5).reshape(S5_BLOCKS, T, P, 2, S5_HALF)
        v = jnp.stack([one(car), -one(cai)], axis=4).reshape(S5_BLOCKS, T, P, 4 * S5_HALF).astype(BF16)
        full = v[:, :, None, :, :] * grp_all[None, None, :, None, :]
        return full.reshape(S5_BLOCKS, T * LANES, 4 * S5_HALF)

    jj = jnp.arange(T)
    b_f = b_mat(ops[0][2], ops[0][3], T - 1 - jj)
    b_r = b_mat(ops[1][2], ops[1][3], jj)
    c_f = c_mat_t(ops[0][5], ops[0][6], jj + 1)
    c_r = c_mat_t(ops[1][5], ops[1][6], T - jj)

    def at(pr, pi):
        r = pr[T].reshape(S5_BLOCKS, 2, 1, S5_HALF)
        i = pi[T].reshape(S5_BLOCKS, 2, 1, S5_HALF)
        return jnp.concatenate([r, i], axis=-1)

    return dict(m=m_op, b_f=b_f, b_r=b_r, c_f=c_f, c_r=c_r,
                at_f=at(ops[0][0], ops[0][1]), at_r=at(ops[1][0], ops[1][1]))


def _cmul_add(a, s, d):
    h = S5_HALF
    ar, ai = a[:, :h], a[:, h:]
    sr, si = s[:, :h], s[:, h:]
    return jnp.concatenate([ar * sr - ai * si, ar * si + ai * sr], axis=1) + d


def _s5_scan_kernel(x_ref, bf_ref, br_ref, atf_ref, atr_ref, if_ref, ir_ref,
                    s_ref, r_ref, ff_ref, fr_ref, *, rows):
    rc = min(rows, 256)
    for r0 in range(0, rows, rc):
        xs = x_ref[r0:r0 + rc, :]
        s_ref[r0:r0 + rc, :] = jnp.dot(xs, bf_ref[...], preferred_element_type=F32)
        r_ref[r0:r0 + rc, :] = jnp.dot(xs, br_ref[...], preferred_element_type=F32)

    ntile = rows // SUBLANES
    atf = atf_ref[...]
    atr = atr_ref[...]
    low = lax.broadcasted_iota(jnp.int32, (SUBLANES, 2 * S5_HALF), 0) < 4

    def body(i, carry):
        s, r = carry
        fo = pl.multiple_of(i * SUBLANES, SUBLANES)
        d = s_ref[pl.ds(fo, SUBLANES), :]
        t1 = pltpu.roll(_cmul_add(atf, s, d), 4, 0)
        t2 = _cmul_add(atf, t1, d)
        s_ref[pl.ds(fo, SUBLANES), :] = jnp.where(low, s, t1)
        ro = pl.multiple_of((ntile - 1 - i) * SUBLANES, SUBLANES)
        e = r_ref[pl.ds(ro, SUBLANES), :]
        u1 = pltpu.roll(_cmul_add(atr, r, e), 4, 0)
        u2 = _cmul_add(atr, u1, e)
        r_ref[pl.ds(ro, SUBLANES), :] = jnp.where(low, u1, r)
        return pltpu.roll(t2, 4, 0), pltpu.roll(u2, 4, 0)

    s0 = if_ref[...]
    r0 = pltpu.roll(ir_ref[...], 4, 0)
    s, r = lax.fori_loop(0, ntile, body, (s0, r0))
    ff_ref[...] = s
    fr_ref[...] = pltpu.roll(r, 4, 0)


def s5_scan(x4, ops, init_f, init_r):
    nb, rows, tk = x4.shape
    w = 2 * S5_HALF
    kern = functools.partial(_s5_scan_kernel, rows=rows)
    st_spec = pl.BlockSpec((None, None, SUBLANES, w), lambda b, h: (b, h, 0, 0))
    out_shape = (
        jax.ShapeDtypeStruct((nb, rows, 2 * w), F32),
        jax.ShapeDtypeStruct((nb, rows, 2 * w), F32),
        jax.ShapeDtypeStruct((nb, 2, SUBLANES, w), F32),
        jax.ShapeDtypeStruct((nb, 2, SUBLANES, w), F32),
    )
    return pl.pallas_call(
        kern,
        grid=(nb, 2),
        in_specs=[
            pl.BlockSpec((None, rows, tk), lambda b, h: (b, 0, 0)),
            pl.BlockSpec((None, None, tk, w), lambda b, h: (b, h, 0, 0)),
            pl.BlockSpec((None, None, tk, w), lambda b, h: (b, h, 0, 0)),
            pl.BlockSpec((None, None, 1, w), lambda b, h: (b, h, 0, 0)),
            pl.BlockSpec((None, None, 1, w), lambda b, h: (b, h, 0, 0)),
            st_spec, st_spec,
        ],
        out_specs=(
            pl.BlockSpec((None, rows, w), lambda b, h: (b, 0, h)),
            pl.BlockSpec((None, rows, w), lambda b, h: (b, 0, h)),
            st_spec, st_spec,
        ),
        out_shape=out_shape,
        compiler_params=_cparams(("parallel", "parallel"), 48),
        name="s5_scan",
    )(x4, ops["b_f"], ops["b_r"], ops["at_f"], ops["at_r"], init_f, init_r)


def _s5_out_kernel(x_ref, s_ref, r_ref, m_ref, cf_ref, cr_ref, o_ref):
    nt = (((1,), (1,)), ((), ()))
    acc = jnp.dot(x_ref[...], m_ref[...], preferred_element_type=F32)
    acc += lax.dot_general(s_ref[...].astype(BF16), cf_ref[...], nt, preferred_element_type=F32)
    acc += lax.dot_general(r_ref[...].astype(BF16), cr_ref[...], nt, preferred_element_type=F32)
    o_ref[...] = acc


def s5_out(x4, s_st, r_st, ops):
    nb, rows, tk = x4.shape
    tr = min(rows, 512)
    w = 4 * S5_HALF
    return pl.pallas_call(
        _s5_out_kernel,
        grid=(nb, rows // tr),
        in_specs=[
            pl.BlockSpec((None, tr, tk), lambda b, i: (b, i, 0)),
            pl.BlockSpec((None, tr, w), lambda b, i: (b, i, 0)),
            pl.BlockSpec((None, tr, w), lambda b, i: (b, i, 0)),
            pl.BlockSpec((None, tk, tk), lambda b, i: (b, 0, 0)),
            pl.BlockSpec((None, tk, w), lambda b, i: (b, 0, 0)),
            pl.BlockSpec((None, tk, w), lambda b, i: (b, 0, 0)),
        ],
        out_specs=pl.BlockSpec((None, tr, tk), lambda b, i: (b, i, 0)),
        out_shape=jax.ShapeDtypeStruct((nb, rows, tk), F32),
        compiler_params=_cparams(("parallel", "parallel"), 48),
        name="s5_out",
    )(x4, s_st, r_st, ops["m"], ops["c_f"], ops["c_r"])


def _to_chunk_rows(u):
    bsz, L, _ = u.shape
    nc = L // S5_T
    x = u.astype(BF16).reshape(bsz, nc, S5_T, S5_BLOCKS, LANES)
    return x.transpose(3, 1, 0, 2, 4).reshape(S5_BLOCKS, nc * bsz, S5_T * LANES)


def _from_chunk_rows(y4, bsz):
    nb, rows, _ = y4.shape
    nc = rows // bsz
    y = y4.reshape(nb, nc, bsz, S5_T, LANES).transpose(2, 1, 3, 0, 4)
    return y.reshape(bsz, nc * S5_T, S5_WIDTH)


def s5_layer(u, ops, init, readout):
    bsz = u.shape[0]
    x4 = _to_chunk_rows(u)
    if init is None:
        z = jnp.zeros((S5_BLOCKS, 2, SUBLANES, 2 * S5_HALF), F32)
        init = (z, z)
    s_st, r_st, fin_f, fin_r = s5_scan(x4, ops, init[0], init[1])
    y = _from_chunk_rows(s5_out(x4, s_st, r_st, ops), bsz) if readout else None
    return y, (fin_f, fin_r)


def _conv3(x, prev_row, next_row, w, first, last):
    n = x.shape[0]
    rows = lax.broadcasted_iota(jnp.int32, x.shape, 0)
    pz = jnp.where(first, 0.0, prev_row)
    nz = jnp.where(last, 0.0, next_row)
    xp = jnp.where(rows == 0, pz, pltpu.roll(x, 1, 0))
    xn = jnp.where(rows == n - 1, nz, pltpu.roll(x, n - 1, 0))
    return w[0:1, :] * xp + w[1:2, :] * x + w[2:3, :] * xn


def _halo_specs(tl, L, width, col):
    tb = tl // SUBLANES
    nb = L // SUBLANES
    cur = pl.BlockSpec((None, tl, width), lambda b, i: (b, i, col))
    prv = pl.BlockSpec((None, SUBLANES, width), lambda b, i: (b, jnp.maximum(i * tb - 1, 0), col))
    nxt = pl.BlockSpec((None, SUBLANES, width), lambda b, i: (b, jnp.minimum((i + 1) * tb, nb - 1), col))
    return [cur, prv, nxt]


def _dft_plan(L):
    n = 2 * L
    n1 = n // HY_N2
    return n, n1, n1 // 2, n1 // 2 + 1


@functools.lru_cache(maxsize=None)
def _dft_consts(L):
    n, n1, h, nk = _dft_plan(L)
    k1 = np.arange(nk)[:, None]
    m1 = np.arange(h)[None, :]
    th = 2.0 * np.pi * k1 * m1 / n1
    wt = np.where((k1 == 0) | (k1 == h), 1.0, 2.0) / n
    k2 = np.arange(HY_N2)[:, None]
    m2 = np.arange(HY_N2)[None, :]
    ph = -2.0 * np.pi * (k2 * m2 / HY_N2)[None] - 2.0 * np.pi * np.arange(nk)[:, None, None] * m2[None] / n
    gr, gi = np.cos(ph), np.sin(ph)
    wf = np.concatenate([gr.transpose(0, 2, 1), gi.transpose(0, 2, 1)], axis=2)
    wi = np.concatenate([gr, -gi], axis=2)
    f = lambda a: np.asarray(a, np.float32)
    return dict(c1=f(np.cos(th)), s1=f(-np.sin(th)),
                ic=f((wt * np.cos(th)).T), isn=f((wt * np.sin(th)).T),
                wf=f(wf).astype(BF16), wi=f(wi).astype(BF16))


HY_RC = 32
W = HY_WIDTH


def _row_chunks(body):
    def step(ci, carry):
        body(pl.multiple_of(ci * HY_RC, HY_RC))
        return carry
    lax.fori_loop(0, W // HY_RC, step, 0)


def _store_operand(a_ref, r0, vr, vi):
    a_ref[pl.ds(r0, HY_RC), :] = vr.astype(BF16)
    a_ref[pl.ds(W + r0, HY_RC), :] = vi.astype(BF16)


def _stage1(x_ref, c_ref, s_ref, k, h, a_ref):
    def body(r0):
        x0 = x_ref[0, pl.ds(r0, HY_RC), :]
        ar = c_ref[k, 0] * x0
        ai = s_ref[k, 0] * x0
        for m in range(1, h):
            xm = x_ref[m, pl.ds(r0, HY_RC), :]
            ar += c_ref[k, m] * xm
            ai += s_ref[k, m] * xm
        _store_operand(a_ref, r0, ar, ai)
    _row_chunks(body)


def _cmatmul(a_ref, w, r_ref):
    r_ref[...] = jnp.dot(a_ref[...], w, preferred_element_type=F32)


def _cresult(r_ref, r0):
    top = r_ref[pl.ds(r0, HY_RC), :]
    bot = r_ref[pl.ds(W + r0, HY_RC), :]
    return top[:, :HY_N2] - bot[:, HY_N2:], top[:, HY_N2:] + bot[:, :HY_N2]


def _hy_pre_kernel(v_ref, vp_ref, vn_ref, a_ref, ap_ref, an_ref, z_ref, zp_ref, zn_ref,
                   w_ref, b_ref, mt_ref, x0_ref):
    i = pl.program_id(1)
    first = i == 0
    last = i == pl.num_programs(1) - 1
    w = w_ref[...]
    b = b_ref[...]

    def part(c, p, n, k):
        sl = slice(k * HY_WIDTH, (k + 1) * HY_WIDTH)
        return _conv3(c[...], p[SUBLANES - 1:SUBLANES, :], n[0:1, :], w[:, sl], first, last) + b[:, sl]

    v = part(v_ref, vp_ref, vn_ref, 0)
    x1 = part(a_ref, ap_ref, an_ref, 1)
    mt_ref[...] = (x1 * v).T
    x0_ref[...] = part(z_ref, zp_ref, zn_ref, 2)


def hy_pre(p, conv_w, conv_b):
    bsz, L, _ = p.shape
    tl = ROW_TILE
    c0 = OFF_HY // HY_WIDTH
    specs = []
    for k in range(3):
        specs += _halo_specs(tl, L, HY_WIDTH, c0 + k)
    specs += [pl.BlockSpec((3, 3 * HY_WIDTH), lambda b, i: (0, 0)),
              pl.BlockSpec((1, 3 * HY_WIDTH), lambda b, i: (0, 0))]
    return pl.pallas_call(
        _hy_pre_kernel,
        grid=(bsz, L // tl),
        in_specs=specs,
        out_specs=(pl.BlockSpec((None, None, HY_WIDTH, HY_N2), lambda b, i: (b, i, 0, 0)),
                   pl.BlockSpec((None, tl, HY_WIDTH), lambda b, i: (b, i, 0))),
        out_shape=(jax.ShapeDtypeStruct((bsz, L // HY_N2, HY_WIDTH, HY_N2), F32),
                   jax.ShapeDtypeStruct((bsz, L, HY_WIDTH), F32)),
        compiler_params=_cparams(("parallel", "parallel"), 32),
        name="hy_pre",
    )(*([p] * 9), conv_w, conv_b.reshape(1, -1))


def _hy_filter_kernel(z_ref, w1_ref, b1_ref, w2_ref, b2_ref, w3_ref, b3_ref, fr_ref, w4_ref,
                      dl_ref, hf_ref, hb_ref, nrm_ref, *, tl):
    i = pl.program_id(0)
    z = z_ref[...]
    fr = fr_ref[...]
    h = jnp.sin(fr * (_hdot(z, w1_ref[...]) + b1_ref[...]))
    h = jnp.sin(fr * (_hdot(h, w2_ref[...]) + b2_ref[...]))
    h = jnp.sin(fr * (_hdot(h, w3_ref[...]) + b3_ref[...]))
    h = _hdot(h, w4_ref[...])
    t = z[:, 0:1]
    window = jnp.exp(-t * dl_ref[...]) + HY_DECAY_SHIFT
    hf = h[:, :HY_WIDTH] * window
    rows = lax.broadcasted_iota(jnp.int32, (tl, HY_WIDTH), 0) + i * tl
    hb = jnp.where(rows == 0, 0.0, h[:, HY_WIDTH:] * window)
    hf_ref[...] = hf.T
    hb_ref[...] = hb.T
    part = jnp.sum(jnp.abs(hf) + jnp.abs(hb), axis=0, keepdims=True)

    @pl.when(i == 0)
    def _():
        nrm_ref[...] = jnp.zeros_like(nrm_ref)

    nrm_ref[...] += part


def hy_filter_taps(L, lp):
    t = jnp.linspace(0.0, 1.0, L, dtype=F32)[:, None]
    ang = 2.0 * math.pi * jnp.arange(L, dtype=F32)[:, None] / L
    bands = jnp.linspace(1e-4, HY_BANDS - 1, HY_BANDS, dtype=F32)[None, :]
    z = jnp.concatenate([t, jnp.cos(bands * ang), -jnp.sin(bands * ang),
                         jnp.zeros((L, EMB_PAD - HY_EMB), F32)], axis=-1)
    w1 = jnp.concatenate([lp["hy_f_w1"], jnp.zeros((EMB_PAD - HY_EMB, lp["hy_f_w1"].shape[1]), F32)], axis=0)
    deltas = jnp.abs(jnp.linspace(HY_MIN_DECAY, HY_MAX_DECAY, HY_WIDTH, dtype=F32))[None, :]
    tl = HY_N2
    hid = w1.shape[1]
    full = lambda shape: pl.BlockSpec(shape, lambda i: (0,) * len(shape))
    o_spec = pl.BlockSpec((None, HY_WIDTH, HY_N2), lambda i: (i, 0, 0))
    shp = jax.ShapeDtypeStruct((L // HY_N2, HY_WIDTH, HY_N2), F32)
    return pl.pallas_call(
        functools.partial(_hy_filter_kernel, tl=tl),
        grid=(L // tl,),
        in_specs=[pl.BlockSpec((tl, EMB_PAD), lambda i: (i, 0)),
                  full((EMB_PAD, hid)), full((1, hid)), full((hid, hid)), full((1, hid)),
                  full((hid, hid)), full((1, hid)), full((1, hid)), full((hid, 2 * HY_WIDTH)),
                  full((1, HY_WIDTH))],
        out_specs=(o_spec, o_spec, pl.BlockSpec((1, HY_WIDTH), lambda i: (0, 0))),
        out_shape=(shp, shp, jax.ShapeDtypeStruct((1, HY_WIDTH), F32)),
        compiler_params=_cparams(("arbitrary",), 32),
        name="hy_filter_taps",
    )(z, w1, lp["hy_f_b1"].reshape(1, -1), lp["hy_f_w2"], lp["hy_f_b2"].reshape(1, -1),
      lp["hy_f_w3"], lp["hy_f_b3"].reshape(1, -1), lp["hy_f_freq"].reshape(1, -1), lp["hy_f_w4"], deltas)


def _hy_spectrum_kernel(c_ref, s_ref, hf_ref, hb_ref, w_ref, inv_ref, k_ref,
                        a_ref, rf_ref, rb_ref, *, h):
    k = pl.program_id(0)
    _stage1(hf_ref, c_ref, s_ref, k, h, a_ref)
    _cmatmul(a_ref, w_ref[...], rf_ref)
    _stage1(hb_ref, c_ref, s_ref, k, h, a_ref)
    _cmatmul(a_ref, w_ref[...], rb_ref)

    def body(r0):
        fr, fi = _cresult(rf_ref, r0)
        br, bi = _cresult(rb_ref, r0)
        inv = inv_ref[pl.ds(r0, HY_RC), :]
        k_ref[pl.ds(r0, HY_RC), :] = jnp.concatenate([(fr + br) * inv, (fi - bi) * inv], axis=1)
    _row_chunks(body)


def hy_spectrum(hf_t, hb_t, inv_norm, L):
    n, n1, h, nk = _dft_plan(L)
    cs = _dft_consts(L)
    smem = pl.BlockSpec(memory_space=pltpu.SMEM)
    t_spec = pl.BlockSpec((h, HY_WIDTH, HY_N2), lambda k: (0, 0, 0))
    w_spec = pl.BlockSpec((None, HY_N2, 2 * HY_N2), lambda k: (k, 0, 0))
    return pl.pallas_call(
        functools.partial(_hy_spectrum_kernel, h=h),
        grid=(nk,),
        in_specs=[smem, smem, t_spec, t_spec, w_spec,
                  pl.BlockSpec((HY_WIDTH, 1), lambda k: (0, 0))],
        out_specs=pl.BlockSpec((None, HY_WIDTH, 2 * HY_N2), lambda k: (k, 0, 0)),
        out_shape=jax.ShapeDtypeStruct((nk, HY_WIDTH, 2 * HY_N2), F32),
        scratch_shapes=_hy_scratch(2),
        compiler_params=_cparams(("parallel",), 48),
        name="hy_spectrum",
    )(cs["c1"], cs["s1"], hf_t, hb_t, cs["wf"], inv_norm)


def _hy_scratch(n_results):
    return ([pltpu.VMEM((2 * W, HY_N2), BF16)]
            + [pltpu.VMEM((2 * W, 2 * HY_N2), F32)] * n_results)


def _hy_conv_kernel(c_ref, s_ref, ic_ref, is_ref, m_ref, kf_ref, wf_ref, wi_ref,
                    d_ref, o_ref, a_ref, r_ref, *, h):
    k = pl.program_id(1)
    _stage1(m_ref, c_ref, s_ref, k, h, a_ref)
    _cmatmul(a_ref, wf_ref[...], r_ref)

    def spectrum_product(r0):
        yr, yi = _cresult(r_ref, r0)
        kf = kf_ref[pl.ds(r0, HY_RC), :]
        kr, ki = kf[:, :HY_N2], kf[:, HY_N2:]
        _store_operand(a_ref, r0, yr * kr - yi * ki, yr * ki + yi * kr)
    _row_chunks(spectrum_product)
    _cmatmul(a_ref, wi_ref[...], r_ref)

    @pl.when(k == 0)
    def _():
        def body(r0):
            br, bi = _cresult(r_ref, r0)
            d = d_ref[pl.ds(r0, HY_RC), :]
            for m in range(h):
                o_ref[m, pl.ds(r0, HY_RC), :] = (m_ref[m, pl.ds(r0, HY_RC), :] * d
                                                 + (ic_ref[m, 0] * br - is_ref[m, 0] * bi))
        _row_chunks(body)

    @pl.when(k > 0)
    def _():
        def body(r0):
            br, bi = _cresult(r_ref, r0)
            for m in range(h):
                o_ref[m, pl.ds(r0, HY_RC), :] += ic_ref[m, k] * br - is_ref[m, k] * bi
        _row_chunks(body)


def hy_conv(m_t, kf, d_col, L):
    n, n1, h, nk = _dft_plan(L)
    cs = _dft_consts(L)
    bsz = m_t.shape[0]
    smem = pl.BlockSpec(memory_space=pltpu.SMEM)
    t_spec = pl.BlockSpec((None, h, HY_WIDTH, HY_N2), lambda b, k: (b, 0, 0, 0))
    w_spec = pl.BlockSpec((None, HY_N2, 2 * HY_N2), lambda b, k: (k, 0, 0))
    return pl.pallas_call(
        functools.partial(_hy_conv_kernel, h=h),
        grid=(bsz, nk),
        in_specs=[smem, smem, smem, smem, t_spec,
                  pl.BlockSpec((None, HY_WIDTH, 2 * HY_N2), lambda b, k: (k, 0, 0)),
                  w_spec, w_spec,
                  pl.BlockSpec((HY_WIDTH, 1), lambda b, k: (0, 0))],
        out_specs=t_spec,
        out_shape=jax.ShapeDtypeStruct(m_t.shape, F32),
        scratch_shapes=_hy_scratch(1),
        compiler_params=_cparams(("parallel", "arbitrary"), 52),
        name="hy_conv",
    )(cs["c1"], cs["s1"], cs["ic"], cs["isn"], m_t, kf, cs["wf"], cs["wi"], d_col)


def hyena_mix(p, lp):
    L = p.shape[1]
    hf_t, hb_t, nrm = hy_filter_taps(L, lp)
    kf = hy_spectrum(hf_t, hb_t, (1.0 / nrm).reshape(HY_WIDTH, 1), L)
    m_t, x0 = hy_pre(p, lp["hy_conv_w"], lp["hy_conv_b"])
    z_t = hy_conv(m_t, kf, lp["hy_d"].astype(F32).reshape(HY_WIDTH, 1), L)
    return z_t, x0


def _gelu_tanh(x):
    return 0.5 * x * (1.0 + jnp.tanh(math.sqrt(2.0 / math.pi) * (x + 0.044715 * (x * x * x))))


def _merge_kernel(x_ref, y5_ref, zt_ref, x0_ref,
                  sx_ref, sxp_ref, sxn_ref, sb_ref, sc_ref, scp_ref, scn_ref,
                  g0a_ref, g0b_ref, g1a_ref, g1b_ref, g2a_ref, g2b_ref, mg_ref,
                  glu_ref, scw_ref, sco_ref, hyo_ref, ow_ref, o_ref):
    i = pl.program_id(1)
    first = i == 0
    last = i == pl.num_programs(1) - 1
    ag = _bdot(_gelu_tanh(y5_ref[...]), glu_ref[...])
    y_s5 = ag[:, :D_MODEL] * jax.nn.sigmoid(ag[:, D_MODEL:])
    l7 = slice(SUBLANES - 1, SUBLANES)
    cx = sc_ref[...] * sx_ref[...]
    cxp = scp_ref[l7, :] * sxp_ref[l7, :]
    cxn = scn_ref[0:1, :] * sxn_ref[0:1, :]
    y_sc = _bdot(sb_ref[...] * _conv3(cx, cxp, cxn, scw_ref[...], first, last), sco_ref[...])
    y_hy = _bdot(x0_ref[...] * zt_ref[...].T, hyo_ref[...])
    gate = lambda a, b: jax.nn.sigmoid(jnp.concatenate([a[...], b[...]], axis=1))
    m = gate(g0a_ref, g0b_ref) * y_s5 + gate(g1a_ref, g1b_ref) * y_sc + gate(g2a_ref, g2b_ref) * y_hy
    o_ref[...] = x_ref[...] + mg_ref[...] * _bdot(m, ow_ref[...])


def merge(x, p, y5, z_t, x0, mod3, wl, ctx):
    bsz, L, _ = x.shape
    tl = ROW_TILE
    row = lambda width, col: pl.BlockSpec((None, tl, width), lambda b, i: (b, i, col))
    full = lambda shape: pl.BlockSpec(shape, lambda b, i: (0,) * len(shape))
    c_sc = OFF_SC // SC_WIDTH
    half = D_MODEL // 2
    c_g = OFF_GATE // half
    specs = [row(D_MODEL, 0), row(S5_WIDTH, 0),
             pl.BlockSpec((None, None, HY_WIDTH, HY_N2), lambda b, i: (b, i, 0, 0)), row(HY_WIDTH, 0)]
    specs += _halo_specs(tl, L, SC_WIDTH, c_sc)
    specs += [row(SC_WIDTH, c_sc + 1)]
    specs += _halo_specs(tl, L, SC_WIDTH, c_sc + 2)
    specs += [row(half, c_g + k) for k in range(2 * N_BRANCH)] + [_mod_spec(2, ctx)]
    specs += [full((S5_WIDTH, 2 * D_MODEL)), full((3, SC_WIDTH)), full((SC_WIDTH, D_MODEL)),
              full((HY_WIDTH, D_MODEL)), full((D_MODEL, D_MODEL))]
    return pl.pallas_call(
        _merge_kernel,
        grid=(bsz, L // tl),
        in_specs=specs,
        out_specs=row(D_MODEL, 0),
        out_shape=jax.ShapeDtypeStruct((bsz, L, D_MODEL), F32),
        compiler_params=_cparams(("parallel", "parallel"), 48),
        name="merge",
    )(x, y5, z_t, x0, *([p] * 13), mod3,
      wl["s5_glu_w"], wl["sc_conv_w"], wl["sc_out_w"], wl["hy_out_w"], wl["out_w"])


def _mlp_kernel(x_ref, g_ref, sh_ref, sc_ref, mg_ref, w1_ref, w2_ref, fg_ref, o_ref, h_ref, acc_ref,
                *, final_norm):
    j = pl.program_id(2)

    @pl.when(j == 0)
    def _():
        h_ref[...] = _norm_mod(x_ref[...], g_ref[...], sh_ref[...], sc_ref[...]).astype(BF16)
        acc_ref[...] = jnp.zeros_like(acc_ref)

    r = jnp.maximum(jnp.dot(h_ref[...], w1_ref[...], preferred_element_type=F32), 0.0)
    acc_ref[...] += _bdot(r * r, w2_ref[...])

    @pl.when(j == pl.num_programs(2) - 1)
    def _():
        y = x_ref[...] + mg_ref[...] * acc_ref[...]
        if final_norm:
            y = (y * lax.rsqrt(jnp.mean(y * y, axis=-1, keepdims=True) + EPS)) * fg_ref[...]
        o_ref[...] = y


def mlp(x, g, mod3, w1, w2, final_g, ctx, final_norm):
    bsz, L, _ = x.shape
    tl = min(L, 1024)
    tf = 1024
    x_spec = pl.BlockSpec((None, tl, D_MODEL), lambda b, i, j: (b, i, 0))
    vec = pl.BlockSpec((1, D_MODEL), lambda b, i, j: (0, 0))
    return pl.pallas_call(
        functools.partial(_mlp_kernel, final_norm=final_norm),
        grid=(bsz, L // tl, D_FF // tf),
        in_specs=[x_spec, vec, _mod_spec(3, ctx), _mod_spec(4, ctx), _mod_spec(5, ctx),
                  pl.BlockSpec((D_MODEL, tf), lambda b, i, j: (0, j)),
                  pl.BlockSpec((tf, D_MODEL), lambda b, i, j: (j, 0)),
                  vec],
        out_specs=x_spec,
        out_shape=jax.ShapeDtypeStruct((bsz, L, D_MODEL), F32),
        scratch_shapes=[pltpu.VMEM((tl, D_MODEL), BF16), pltpu.VMEM((tl, D_MODEL), F32)],
        compiler_params=_cparams(("parallel", "parallel", "arbitrary"), 48),
        name="mlp",
    )(x, g.reshape(1, D_MODEL), mod3, mod3, mod3, w1, w2, final_g.reshape(1, D_MODEL))


def _sincos_2d(rows, cols, dim):
    quarter = dim // 4
    omega = 1.0 / (10000.0 ** (jnp.arange(quarter, dtype=F32) / quarter))
    er = jnp.arange(rows, dtype=F32)[:, None] * omega[None]
    ec = jnp.arange(cols, dtype=F32)[:, None] * omega[None]
    er = jnp.concatenate([jnp.sin(er), jnp.cos(er)], axis=-1)
    ec = jnp.concatenate([jnp.sin(ec), jnp.cos(ec)], axis=-1)
    emb = jnp.concatenate([
        jnp.broadcast_to(er[:, None, :], (rows, cols, dim // 2)),
        jnp.broadcast_to(ec[None, :, :], (rows, cols, dim // 2))], axis=-1)
    return emb.reshape(rows * cols, dim)


def kernel(x, c, ctx, c_ctx, ada_w, ada_b, norm1_g, norm2_g, w_in, s5_a_re, s5_a_im, s5_log_dt, s5_b_re, s5_b_im, s5_c_re, s5_c_im, s5_d, s5_glu_w, sc_conv_w, sc_out_w, hy_conv_w, hy_conv_b, hy_f_w1, hy_f_b1, hy_f_w2, hy_f_b2, hy_f_w3, hy_f_b3, hy_f_freq, hy_f_w4, hy_d, hy_out_w, out_w, mlp_w1, mlp_w2, final_g):
    bsz, L, _ = x.shape
    x = x + _sincos_2d(L // GRID_W, GRID_W, D_MODEL)[None]
    xc = ctx
    cc = jnp.concatenate([c, c_ctx[None, :], jnp.zeros((MOD_ROWS - bsz - 1, D_MODEL), F32)], axis=0)
    mod_all = ada_mod(cc, ada_w, ada_b)
    for l in range(DEPTH):
        last = l == DEPTH - 1
        mod3 = mod_all[l].reshape(MOD_ROWS, 1, N_MOD * D_MODEL)
        w_in_l = w_in[l].astype(BF16)
        wl = {"s5_glu_w": s5_glu_w[l].astype(BF16), "sc_conv_w": sc_conv_w[l],
              "sc_out_w": sc_out_w[l].astype(BF16), "hy_out_w": hy_out_w[l].astype(BF16),
              "out_w": out_w[l].astype(BF16)}
        lp = {"hy_conv_w": hy_conv_w[l], "hy_conv_b": hy_conv_b[l],
              "hy_f_w1": hy_f_w1[l], "hy_f_b1": hy_f_b1[l], "hy_f_w2": hy_f_w2[l], "hy_f_b2": hy_f_b2[l],
              "hy_f_w3": hy_f_w3[l], "hy_f_b3": hy_f_b3[l], "hy_f_freq": hy_f_freq[l],
              "hy_f_w4": hy_f_w4[l], "hy_d": hy_d[l]}
        w1 = mlp_w1[l].astype(BF16)
        w2 = mlp_w2[l].astype(BF16)
        ops = _s5_operators(s5_a_re[l], s5_a_im[l], s5_log_dt[l], s5_b_re[l], s5_b_im[l],
                            s5_c_re[l], s5_c_im[l], s5_d[l])

        pc = norm_mm(xc, norm1_g[l], mod3, w_in_l[:, :S5_WIDTH] if last else w_in_l, ctx=True)
        yc5, ctx_final = s5_layer(pc[..., OFF_S5:OFF_S5 + S5_WIDTH], ops, None, not last)

        p = norm_mm(x, norm1_g[l], mod3, w_in_l, ctx=False)
        y5, _ = s5_layer(p[..., OFF_S5:OFF_S5 + S5_WIDTH], ops, ctx_final, True)
        x = merge(x, p, y5, *hyena_mix(p, lp), mod3, wl, ctx=False)
        x = mlp(x, norm2_g[l], mod3, w1, w2, final_g, ctx=False, final_norm=last)

        if not last:
            xc = merge(xc, pc, yc5, *hyena_mix(pc, lp), mod3, wl, ctx=True)
            xc = mlp(xc, norm2_g[l], mod3, w1, w2, final_g, ctx=True, final_norm=False)
    return x
```

```python
import functools
import math

import numpy as np
import jax
import jax.numpy as jnp
from jax import lax
from jax.experimental import pallas as pl
from jax.experimental.pallas import tpu as pltpu

F32 = jnp.float32
BF16 = jnp.bfloat16
HIGHEST = lax.Precision.HIGHEST

D_MODEL = 1024
DEPTH = 2
GRID_W = 64
S5_WIDTH = 512
S5_GROUP = 16
S5_GROUPS = S5_WIDTH // S5_GROUP
S5_STATE = 64
SC_WIDTH = 512
HY_WIDTH = 512
HY_BANDS = 16
HY_EMB = 1 + 2 * HY_BANDS
HY_FAST_DECAY = 0.3
HY_SLOW_DECAY = 1.5
HY_DECAY_TARGET = 1e-2
HY_DECAY_SHIFT = 0.05
HY_MAX_DECAY = math.log(HY_DECAY_TARGET) / HY_FAST_DECAY
HY_MIN_DECAY = math.log(HY_DECAY_TARGET) / HY_SLOW_DECAY
N_BRANCH = 3
D_FF = 4 * D_MODEL
N_MOD = 6
EPS = 1e-6
OFF_S5 = 0
OFF_SC = OFF_S5 + S5_WIDTH
OFF_HY = OFF_SC + 3 * SC_WIDTH
OFF_GATE = OFF_HY + 3 * HY_WIDTH
D_IN = OFF_GATE + N_BRANCH * D_MODEL

LANES = 128
SUBLANES = 8
S5_T = 8
S5_BLOCKS = S5_WIDTH // LANES
S5_HALF = 256
EMB_PAD = 40
MOD_ROWS = 8
CTX_ROW = 4
HY_N2 = 256
ROW_TILE = 256


def _cparams(sem, vmem_mb):
    return pltpu.CompilerParams(dimension_semantics=sem, vmem_limit_bytes=vmem_mb * 1024 * 1024)


def _bdot(a, b):
    return jnp.dot(a.astype(BF16), b.astype(BF16), preferred_element_type=F32)


def _hdot(a, b):
    return jnp.dot(a, b, precision=HIGHEST, preferred_element_type=F32)


def _ada_kernel(c_ref, w_ref, b_ref, o_ref):
    c = c_ref[...]
    s = c * jax.nn.sigmoid(c)
    o_ref[...] = _bdot(s, w_ref[...]) + b_ref[...]


def ada_mod(cc, ada_w, ada_b):
    n = N_MOD * D_MODEL
    tn = 1536
    return pl.pallas_call(
        _ada_kernel,
        grid=(DEPTH, n // tn),
        in_specs=[
            pl.BlockSpec((MOD_ROWS, D_MODEL), lambda l, j: (0, 0)),
            pl.BlockSpec((None, D_MODEL, tn), lambda l, j: (l, 0, j)),
            pl.BlockSpec((None, 1, tn), lambda l, j: (l, 0, j)),
        ],
        out_specs=pl.BlockSpec((None, MOD_ROWS, tn), lambda l, j: (l, 0, j)),
        out_shape=jax.ShapeDtypeStruct((DEPTH, MOD_ROWS, n), F32),
        compiler_params=_cparams(("parallel", "parallel"), 40),
        name="ada_mod",
    )(cc, ada_w, ada_b.reshape(DEPTH, 1, n))


def _mod_spec(chunk, ctx):
    if ctx:
        return pl.BlockSpec((None, 1, D_MODEL), lambda b, *_: (CTX_ROW, 0, chunk))
    return pl.BlockSpec((None, 1, D_MODEL), lambda b, *_: (b, 0, chunk))


def _norm_mod(x, g, sh, sc):
    y = x * lax.rsqrt(jnp.mean(x * x, axis=-1, keepdims=True) + EPS)
    return (y * g) * (1.0 + sc) + sh


def _norm_mm_kernel(x_ref, g_ref, sh_ref, sc_ref, w_ref, o_ref, h_ref):
    @pl.when(pl.program_id(2) == 0)
    def _():
        h_ref[...] = _norm_mod(x_ref[...], g_ref[...], sh_ref[...], sc_ref[...]).astype(BF16)

    o_ref[...] = jnp.dot(h_ref[...], w_ref[...], preferred_element_type=F32)


def norm_mm(x, g, mod3, w, ctx):
    bsz, L, _ = x.shape
    n = w.shape[1]
    tl = min(L, 1024)
    tn = n // 4 if n % (4 * LANES) == 0 and n > 2048 else n
    return pl.pallas_call(
        _norm_mm_kernel,
        grid=(bsz, L // tl, n // tn),
        in_specs=[
            pl.BlockSpec((None, tl, D_MODEL), lambda b, i, j: (b, i, 0)),
            pl.BlockSpec((1, D_MODEL), lambda b, i, j: (0, 0)),
            _mod_spec(0, ctx),
            _mod_spec(1, ctx),
            pl.BlockSpec((D_MODEL, tn), lambda b, i, j: (0, j)),
        ],
        out_specs=pl.BlockSpec((None, tl, tn), lambda b, i, j: (b, i, j)),
        out_shape=jax.ShapeDtypeStruct((bsz, L, n), F32),
        scratch_shapes=[pltpu.VMEM((tl, D_MODEL), BF16)],
        compiler_params=_cparams(("parallel", "parallel", "arbitrary"), 48),
        name="norm_mm",
    )(x, g.reshape(1, D_MODEL), mod3, mod3, w)


@functools.lru_cache(maxsize=None)
def _s5_spread_consts():
    T, P = S5_T, S5_GROUP
    src = np.arange(T * P)
    dst = np.arange(T * LANES)
    spread = (src[:, None] // P == dst[None, :] // LANES) & (src[:, None] % P == dst[None, :] % P)
    grp = (dst // P) % 8
    same = grp[:, None] == grp[None, :]
    return spread.astype(np.float32).astype(BF16), same.astype(np.float32).astype(BF16)


def _s5_operators(a_re, a_im, log_dt, b_re, b_im, c_re, c_im, d_skip):
    T = S5_T
    G, N, P = S5_GROUPS, S5_STATE, S5_GROUP
    taus = jnp.arange(T + 1, dtype=F32)[:, None, None]
    ops = []
    for k in range(2):
        ar, ai = a_re[k].astype(F32), a_im[k].astype(F32)
        dt = jnp.exp(log_dt[k].astype(F32))[:, None]
        mag = jnp.exp(ar * dt * taus)
        ang = ai * dt * taus
        pr, pi = mag * jnp.cos(ang), mag * jnp.sin(ang)
        den = ar * ar + ai * ai
        nr, ni = pr[1] - 1.0, pi[1]
        f_re = (nr * ar + ni * ai) / den
        f_im = (ni * ar - nr * ai) / den
        br, bi = b_re[k].astype(F32), b_im[k].astype(F32)
        bbr = f_re[..., None] * br - f_im[..., None] * bi
        bbi = f_re[..., None] * bi + f_im[..., None] * br
        er = pr[..., None] * bbr - pi[..., None] * bbi
        ei = pr[..., None] * bbi + pi[..., None] * bbr
        cr, ci = c_re[k].astype(F32), c_im[k].astype(F32)
        kk = (jnp.einsum("gqn,tgnp->tgqp", cr, er[:T], precision=HIGHEST)
              - jnp.einsum("gqn,tgnp->tgqp", ci, ei[:T], precision=HIGHEST))
        car = cr[None] * pr[:, :, None, :] - ci[None] * pi[:, :, None, :]
        cai = cr[None] * pi[:, :, None, :] + ci[None] * pr[:, :, None, :]
        ops.append((pr, pi, er, ei, kk, car, cai))

    j_in = jnp.arange(T)[:, None]
    j_out = jnp.arange(T)[None, :]
    tau = (j_out - j_in)[..., None, None, None]
    dterm = jnp.eye(P, dtype=F32)[None] * d_skip.astype(F32).reshape(G, 1, P)
    ksum = (jnp.where(tau >= 0, ops[0][4][jnp.clip(j_out - j_in, 0, T - 1)], 0.0)
            + jnp.where(tau <= 0, ops[1][4][jnp.clip(j_in - j_out, 0, T - 1)], 0.0)
            + jnp.where(tau == 0, dterm[None, None], 0.0))
    ks = ksum.reshape(T, T, S5_BLOCKS, 8, P, P).transpose(2, 0, 3, 5, 1, 4)
    ks = ks.reshape(S5_BLOCKS, T * LANES, T * P).astype(BF16)
    spread, same_group = _s5_spread_consts()
    m_op = jnp.einsum("brk,kc->brc", ks, spread, preferred_element_type=BF16) * same_group[None]

    lane = jnp.arange(2 * S5_HALF)
    grp_half = (jnp.arange(8)[None, :, None]
                == jnp.arange(2)[:, None, None] * 4 + ((lane % S5_HALF) // N)[None, None, :]).astype(BF16)
    grp_all = jnp.concatenate([grp_half[0], grp_half[1]], axis=-1)
    row_group = (np.arange(T * LANES) // P) % 8

    def b_mat(er, ei, sel):
        def one(e):
            e6 = e[sel].reshape(T, S5_BLOCKS, 2, 4, N, P)
            return e6.transpose(1, 2, 0, 5, 3, 4).reshape(S5_BLOCKS, 2, T, P, S5_HALF)
        v = jnp.concatenate([one(er), one(ei)], axis=-1).astype(BF16)
        v = v.reshape(S5_BLOCKS, 2, T * P, 2 * S5_HALF)
        rows = jnp.einsum("rk,bhkc->bhrc", spread.T, v, preferred_element_type=BF16)
        return rows * grp_half[:, row_group, :][None]

    def c_mat_t(car, cai, sel):
        def one(c):
            c6 = c[sel].reshape(T, S5_BLOCKS, 2, 4, P, N)
            return c6.transpose(1, 0, 4, 2, 3, 5).reshape(S5_BLOCKS, T, P, 2, S5_HALF)
        v = jnp.stack([one(car), -one(cai)], axis=4).reshape(S5_BLOCKS, T * P, 4 * S5_HALF).astype(BF16)
        rows = jnp.einsum("rk,bkc->brc", spread.T, v, preferred_element_type=BF16)
        return rows * grp_all[row_group, :][None]

    jj = jnp.arange(T)
    b_f = b_mat(ops[0][2], ops[0][3], T - 1 - jj)
    b_r = b_mat(ops[1][2], ops[1][3], jj)
    c_f = c_mat_t(ops[0][5], ops[0][6], jj + 1)
    c_r = c_mat_t(ops[1][5], ops[1][6], T - jj)

    def at(pr, pi):
        r = pr[T].reshape(S5_BLOCKS, 2, 1, S5_HALF)
        i = pi[T].reshape(S5_BLOCKS, 2, 1, S5_HALF)
        return jnp.concatenate([r, i], axis=-1)

    return dict(m=m_op, b_f=b_f, b_r=b_r, c_f=c_f, c_r=c_r,
                at_f=at(ops[0][0], ops[0][1]), at_r=at(ops[1][0], ops[1][1]))


def _cmul_add(a, s, d):
    h = S5_HALF
    ar, ai = a[:, :h], a[:, h:]
    sr, si = s[:, :h], s[:, h:]
    return jnp.concatenate([ar * sr - ai * si, ar * si + ai * sr], axis=1) + d


def _s5_scan_kernel(x_ref, bf_ref, br_ref, atf_ref, atr_ref, if_ref, ir_ref,
                    s_ref, r_ref, ff_ref, fr_ref, *, rows):
    rc = min(rows, 256)
    for r0 in range(0, rows, rc):
        xs = x_ref[r0:r0 + rc, :]
        s_ref[r0:r0 + rc, :] = jnp.dot(xs, bf_ref[...], preferred_element_type=F32)
        r_ref[r0:r0 + rc, :] = jnp.dot(xs, br_ref[...], preferred_element_type=F32)

    ntile = rows // SUBLANES
    atf = atf_ref[...]
    atr = atr_ref[...]
    low = lax.broadcasted_iota(jnp.int32, (SUBLANES, 2 * S5_HALF), 0) < 4

    def body(i, carry):
        s, r = carry
        fo = pl.multiple_of(i * SUBLANES, SUBLANES)
        d = s_ref[pl.ds(fo, SUBLANES), :]
        t1 = pltpu.roll(_cmul_add(atf, s, d), 4, 0)
        t2 = _cmul_add(atf, t1, d)
        s_ref[pl.ds(fo, SUBLANES), :] = jnp.where(low, s, t1)
        ro = pl.multiple_of((ntile - 1 - i) * SUBLANES, SUBLANES)
        e = r_ref[pl.ds(ro, SUBLANES), :]
        u1 = pltpu.roll(_cmul_add(atr, r, e), 4, 0)
        u2 = _cmul_add(atr, u1, e)
        r_ref[pl.ds(ro, SUBLANES), :] = jnp.where(low, u1, r)
        return pltpu.roll(t2, 4, 0), pltpu.roll(u2, 4, 0)

    s0 = if_ref[...]
    r0 = pltpu.roll(ir_ref[...], 4, 0)
    s, r = lax.fori_loop(0, ntile, body, (s0, r0))
    ff_ref[...] = s
    fr_ref[...] = pltpu.roll(r, 4, 0)


def s5_scan(x4, ops, init_f, init_r):
    nb, rows, tk = x4.shape
    w = 2 * S5_HALF
    kern = functools.partial(_s5_scan_kernel, rows=rows)
    st_spec = pl.BlockSpec((None, None, SUBLANES, w), lambda b, h: (b, h, 0, 0))
    out_shape = (
        jax.ShapeDtypeStruct((nb, rows, 2 * w), F32),
        jax.ShapeDtypeStruct((nb, rows, 2 * w), F32),
        jax.ShapeDtypeStruct((nb, 2, SUBLANES, w), F32),
        jax.ShapeDtypeStruct((nb, 2, SUBLANES, w), F32),
    )
    return pl.pallas_call(
        kern,
        grid=(nb, 2),
        in_specs=[
            pl.BlockSpec((None, rows, tk), lambda b, h: (b, 0, 0)),
            pl.BlockSpec((None, None, tk, w), lambda b, h: (b, h, 0, 0)),
            pl.BlockSpec((None, None, tk, w), lambda b, h: (b, h, 0, 0)),
            pl.BlockSpec((None, None, 1, w), lambda b, h: (b, h, 0, 0)),
            pl.BlockSpec((None, None, 1, w), lambda b, h: (b, h, 0, 0)),
            st_spec, st_spec,
        ],
        out_specs=(
            pl.BlockSpec((None, rows, w), lambda b, h: (b, 0, h)),
            pl.BlockSpec((None, rows, w), lambda b, h: (b, 0, h)),
            st_spec, st_spec,
        ),
        out_shape=out_shape,
        compiler_params=_cparams(("parallel", "parallel"), 48),
        name="s5_scan",
    )(x4, ops["b_f"], ops["b_r"], ops["at_f"], ops["at_r"], init_f, init_r)


def _s5_out_kernel(x_ref, s_ref, r_ref, m_ref, cf_ref, cr_ref, o_ref):
    nt = (((1,), (1,)), ((), ()))
    acc = jnp.dot(x_ref[...], m_ref[...], preferred_element_type=F32)
    acc += lax.dot_general(s_ref[...].astype(BF16), cf_ref[...], nt, preferred_element_type=F32)
    acc += lax.dot_general(r_ref[...].astype(BF16), cr_ref[...], nt, preferred_element_type=F32)
    o_ref[...] = acc


def s5_out(x4, s_st, r_st, ops):
    nb, rows, tk = x4.shape
    tr = min(rows, 512)
    w = 4 * S5_HALF
    return pl.pallas_call(
        _s5_out_kernel,
        grid=(nb, rows // tr),
        in_specs=[
            pl.BlockSpec((None, tr, tk), lambda b, i: (b, i, 0)),
            pl.BlockSpec((None, tr, w), lambda b, i: (b, i, 0)),
            pl.BlockSpec((None, tr, w), lambda b, i: (b, i, 0)),
            pl.BlockSpec((None, tk, tk), lambda b, i: (b, 0, 0)),
            pl.BlockSpec((None, tk, w), lambda b, i: (b, 0, 0)),
            pl.BlockSpec((None, tk, w), lambda b, i: (b, 0, 0)),
        ],
        out_specs=pl.BlockSpec((None, tr, tk), lambda b, i: (b, i, 0)),
        out_shape=jax.ShapeDtypeStruct((nb, rows, tk), F32),
        compiler_params=_cparams(("parallel", "parallel"), 48),
        name="s5_out",
    )(x4, s_st, r_st, ops["m"], ops["c_f"], ops["c_r"])


def _to_chunk_rows(u):
    bsz, L, _ = u.shape
    nc = L // S5_T
    x = u.astype(BF16).reshape(bsz, nc, S5_T, S5_BLOCKS, LANES)
    return x.transpose(3, 1, 0, 2, 4).reshape(S5_BLOCKS, nc * bsz, S5_T * LANES)


def _from_chunk_rows(y4, bsz):
    nb, rows, _ = y4.shape
    nc = rows // bsz
    y = y4.reshape(nb, nc, bsz, S5_T, LANES).transpose(2, 1, 3, 0, 4)
    return y.reshape(bsz, nc * S5_T, S5_WIDTH)


def s5_layer(u, ops, init, readout):
    bsz = u.shape[0]
    x4 = _to_chunk_rows(u)
    if init is None:
        z = jnp.zeros((S5_BLOCKS, 2, SUBLANES, 2 * S5_HALF), F32)
        init = (z, z)
    s_st, r_st, fin_f, fin_r = s5_scan(x4, ops, init[0], init[1])
    y = _from_chunk_rows(s5_out(x4, s_st, r_st, ops), bsz) if readout else None
    return y, (fin_f, fin_r)


def _conv3(x, prev_row, next_row, w, first, last):
    n = x.shape[0]
    rows = lax.broadcasted_iota(jnp.int32, x.shape, 0)
    pz = jnp.where(first, 0.0, prev_row)
    nz = jnp.where(last, 0.0, next_row)
    xp = jnp.where(rows == 0, pz, pltpu.roll(x, 1, 0))
    xn = jnp.where(rows == n - 1, nz, pltpu.roll(x, n - 1, 0))
    return w[0:1, :] * xp + w[1:2, :] * x + w[2:3, :] * xn


def _halo_specs(tl, L, width, col):
    tb = tl // SUBLANES
    nb = L // SUBLANES
    cur = pl.BlockSpec((None, tl, width), lambda b, i: (b, i, col))
    prv = pl.BlockSpec((None, SUBLANES, width), lambda b, i: (b, jnp.maximum(i * tb - 1, 0), col))
    nxt = pl.BlockSpec((None, SUBLANES, width), lambda b, i: (b, jnp.minimum((i + 1) * tb, nb - 1), col))
    return [cur, prv, nxt]


def _dft_plan(L):
    n = 2 * L
    n1 = n // HY_N2
    return n, n1, n1 // 2, n1 // 2 + 1


@functools.lru_cache(maxsize=None)
def _dft_consts(L):
    n, n1, h, nk = _dft_plan(L)
    k1 = np.arange(nk)[:, None]
    m1 = np.arange(h)[None, :]
    th = 2.0 * np.pi * k1 * m1 / n1
    wt = np.where((k1 == 0) | (k1 == h), 1.0, 2.0) / n
    k2 = np.arange(HY_N2)[:, None]
    m2 = np.arange(HY_N2)[None, :]
    ph = -2.0 * np.pi * (k2 * m2 / HY_N2)[None] - 2.0 * np.pi * np.arange(nk)[:, None, None] * m2[None] / n
    gr, gi = np.cos(ph), np.sin(ph)
    wf = np.concatenate([gr.transpose(0, 2, 1), gi.transpose(0, 2, 1)], axis=2)
    wi = np.concatenate([gr, -gi], axis=2)
    f = lambda a: np.asarray(a, np.float32)
    return dict(c1=f(np.cos(th)), s1=f(-np.sin(th)),
                ic=f((wt * np.cos(th)).T), isn=f((wt * np.sin(th)).T),
                wf=f(wf).astype(BF16), wi=f(wi).astype(BF16))


HY_RC = 32
W = HY_WIDTH


def _row_chunks(body):
    def step(ci, carry):
        body(pl.multiple_of(ci * HY_RC, HY_RC))
        return carry
    lax.fori_loop(0, W // HY_RC, step, 0)


def _store_operand(a_ref, r0, vr, vi):
    a_ref[pl.ds(r0, HY_RC), :] = vr.astype(BF16)
    a_ref[pl.ds(W + r0, HY_RC), :] = vi.astype(BF16)


def _stage1(x_ref, c_ref, s_ref, k, h, a_ref):
    def body(r0):
        x0 = x_ref[0, pl.ds(r0, HY_RC), :]
        ar = c_ref[k, 0] * x0
        ai = s_ref[k, 0] * x0
        for m in range(1, h):
            xm = x_ref[m, pl.ds(r0, HY_RC), :]
            ar += c_ref[k, m] * xm
            ai += s_ref[k, m] * xm
        _store_operand(a_ref, r0, ar, ai)
    _row_chunks(body)


def _cmatmul(a_ref, w, r_ref):
    r_ref[...] = jnp.dot(a_ref[...], w, preferred_element_type=F32)


def _cresult(r_ref, r0):
    top = r_ref[pl.ds(r0, HY_RC), :]
    bot = r_ref[pl.ds(W + r0, HY_RC), :]
    return top[:, :HY_N2] - bot[:, HY_N2:], top[:, HY_N2:] + bot[:, :HY_N2]


def _hy_pre_kernel(v_ref, vp_ref, vn_ref, a_ref, ap_ref, an_ref, z_ref, zp_ref, zn_ref,
                   w_ref, b_ref, mt_ref, x0_ref):
    i = pl.program_id(1)
    first = i == 0
    last = i == pl.num_programs(1) - 1
    w = w_ref[...]
    b = b_ref[...]

    def part(c, p, n, k):
        sl = slice(k * HY_WIDTH, (k + 1) * HY_WIDTH)
        return _conv3(c[...], p[SUBLANES - 1:SUBLANES, :], n[0:1, :], w[:, sl], first, last) + b[:, sl]

    v = part(v_ref, vp_ref, vn_ref, 0)
    x1 = part(a_ref, ap_ref, an_ref, 1)
    mt_ref[...] = (x1 * v).T
    x0_ref[...] = part(z_ref, zp_ref, zn_ref, 2)


def hy_pre(p, conv_w, conv_b):
    bsz, L, _ = p.shape
    tl = ROW_TILE
    c0 = OFF_HY // HY_WIDTH
    specs = []
    for k in range(3):
        specs += _halo_specs(tl, L, HY_WIDTH, c0 + k)
    specs += [pl.BlockSpec((3, 3 * HY_WIDTH), lambda b, i: (0, 0)),
              pl.BlockSpec((1, 3 * HY_WIDTH), lambda b, i: (0, 0))]
    return pl.pallas_call(
        _hy_pre_kernel,
        grid=(bsz, L // tl),
        in_specs=specs,
        out_specs=(pl.BlockSpec((None, None, HY_WIDTH, HY_N2), lambda b, i: (b, i, 0, 0)),
                   pl.BlockSpec((None, tl, HY_WIDTH), lambda b, i: (b, i, 0))),
        out_shape=(jax.ShapeDtypeStruct((bsz, L // HY_N2, HY_WIDTH, HY_N2), F32),
                   jax.ShapeDtypeStruct((bsz, L, HY_WIDTH), F32)),
        compiler_params=_cparams(("parallel", "parallel"), 32),
        name="hy_pre",
    )(*([p] * 9), conv_w, conv_b.reshape(1, -1))


def _hy_filter_kernel(z_ref, w1_ref, b1_ref, w2_ref, b2_ref, w3_ref, b3_ref, fr_ref, w4_ref,
                      dl_ref, hf_ref, hb_ref, nrm_ref, *, tl):
    i = pl.program_id(0)
    z = z_ref[...]
    fr = fr_ref[...]
    h = jnp.sin(fr * (_hdot(z, w1_ref[...]) + b1_ref[...]))
    h = jnp.sin(fr * (_hdot(h, w2_ref[...]) + b2_ref[...]))
    h = jnp.sin(fr * (_hdot(h, w3_ref[...]) + b3_ref[...]))
    h = _hdot(h, w4_ref[...])
    t = z[:, 0:1]
    window = jnp.exp(-t * dl_ref[...]) + HY_DECAY_SHIFT
    hf = h[:, :HY_WIDTH] * window
    rows = lax.broadcasted_iota(jnp.int32, (tl, HY_WIDTH), 0) + i * tl
    hb = jnp.where(rows == 0, 0.0, h[:, HY_WIDTH:] * window)
    hf_ref[...] = hf.T
    hb_ref[...] = hb.T
    part = jnp.sum(jnp.abs(hf) + jnp.abs(hb), axis=0, keepdims=True)

    @pl.when(i == 0)
    def _():
        nrm_ref[...] = jnp.zeros_like(nrm_ref)

    nrm_ref[...] += part


def hy_filter_taps(L, lp):
    t = jnp.linspace(0.0, 1.0, L, dtype=F32)[:, None]
    ang = 2.0 * math.pi * jnp.arange(L, dtype=F32)[:, None] / L
    bands = jnp.linspace(1e-4, HY_BANDS - 1, HY_BANDS, dtype=F32)[None, :]
    z = jnp.concatenate([t, jnp.cos(bands * ang), -jnp.sin(bands * ang),
                         jnp.zeros((L, EMB_PAD - HY_EMB), F32)], axis=-1)
    w1 = jnp.concatenate([lp["hy_f_w1"], jnp.zeros((EMB_PAD - HY_EMB, lp["hy_f_w1"].shape[1]), F32)], axis=0)
    deltas = jnp.abs(jnp.linspace(HY_MIN_DECAY, HY_MAX_DECAY, HY_WIDTH, dtype=F32))[None, :]
    tl = HY_N2
    hid = w1.shape[1]
    full = lambda shape: pl.BlockSpec(shape, lambda i: (0,) * len(shape))
    o_spec = pl.BlockSpec((None, HY_WIDTH, HY_N2), lambda i: (i, 0, 0))
    shp = jax.ShapeDtypeStruct((L // HY_N2, HY_WIDTH, HY_N2), F32)
    return pl.pallas_call(
        functools.partial(_hy_filter_kernel, tl=tl),
        grid=(L // tl,),
        in_specs=[pl.BlockSpec((tl, EMB_PAD), lambda i: (i, 0)),
                  full((EMB_PAD, hid)), full((1, hid)), full((hid, hid)), full((1, hid)),
                  full((hid, hid)), full((1, hid)), full((1, hid)), full((hid, 2 * HY_WIDTH)),
                  full((1, HY_WIDTH))],
        out_specs=(o_spec, o_spec, pl.BlockSpec((1, HY_WIDTH), lambda i: (0, 0))),
        out_shape=(shp, shp, jax.ShapeDtypeStruct((1, HY_WIDTH), F32)),
        compiler_params=_cparams(("arbitrary",), 32),
        name="hy_filter_taps",
    )(z, w1, lp["hy_f_b1"].reshape(1, -1), lp["hy_f_w2"], lp["hy_f_b2"].reshape(1, -1),
      lp["hy_f_w3"], lp["hy_f_b3"].reshape(1, -1), lp["hy_f_freq"].reshape(1, -1), lp["hy_f_w4"], deltas)


def _hy_spectrum_kernel(c_ref, s_ref, hf_ref, hb_ref, w_ref, inv_ref, k_ref,
                        a_ref, rf_ref, rb_ref, *, h):
    k = pl.program_id(0)
    _stage1(hf_ref, c_ref, s_ref, k, h, a_ref)
    _cmatmul(a_ref, w_ref[...], rf_ref)
    _stage1(hb_ref, c_ref, s_ref, k, h, a_ref)
    _cmatmul(a_ref, w_ref[...], rb_ref)

    def body(r0):
        fr, fi = _cresult(rf_ref, r0)
        br, bi = _cresult(rb_ref, r0)
        inv = inv_ref[pl.ds(r0, HY_RC), :]
        k_ref[pl.ds(r0, HY_RC), :] = jnp.concatenate([(fr + br) * inv, (fi - bi) * inv], axis=1)
    _row_chunks(body)


def hy_spectrum(hf_t, hb_t, inv_norm, L):
    n, n1, h, nk = _dft_plan(L)
    cs = _dft_consts(L)
    smem = pl.BlockSpec(memory_space=pltpu.SMEM)
    t_spec = pl.BlockSpec((h, HY_WIDTH, HY_N2), lambda k: (0, 0, 0))
    w_spec = pl.BlockSpec((None, HY_N2, 2 * HY_N2), lambda k: (k, 0, 0))
    return pl.pallas_call(
        functools.partial(_hy_spectrum_kernel, h=h),
        grid=(nk,),
        in_specs=[smem, smem, t_spec, t_spec, w_spec,
                  pl.BlockSpec((HY_WIDTH, 1), lambda k: (0, 0))],
        out_specs=pl.BlockSpec((None, HY_WIDTH, 2 * HY_N2), lambda k: (k, 0, 0)),
        out_shape=jax.ShapeDtypeStruct((nk, HY_WIDTH, 2 * HY_N2), F32),
        scratch_shapes=_hy_scratch(2),
        compiler_params=_cparams(("parallel",), 48),
        name="hy_spectrum",
    )(cs["c1"], cs["s1"], hf_t, hb_t, cs["wf"], inv_norm)


def _hy_scratch(n_results):
    return ([pltpu.VMEM((2 * W, HY_N2), BF16)]
            + [pltpu.VMEM((2 * W, 2 * HY_N2), F32)] * n_results)


def _hy_conv_kernel(c_ref, s_ref, ic_ref, is_ref, m_ref, kf_ref, wf_ref, wi_ref,
                    d_ref, o_ref, a_ref, r_ref, *, h):
    k = pl.program_id(1)
    _stage1(m_ref, c_ref, s_ref, k, h, a_ref)
    _cmatmul(a_ref, wf_ref[...], r_ref)

    def spectrum_product(r0):
        yr, yi = _cresult(r_ref, r0)
        kf = kf_ref[pl.ds(r0, HY_RC), :]
        kr, ki = kf[:, :HY_N2], kf[:, HY_N2:]
        _store_operand(a_ref, r0, yr * kr - yi * ki, yr * ki + yi * kr)
    _row_chunks(spectrum_product)
    _cmatmul(a_ref, wi_ref[...], r_ref)

    @pl.when(k == 0)
    def _():
        def body(r0):
            br, bi = _cresult(r_ref, r0)
            d = d_ref[pl.ds(r0, HY_RC), :]
            for m in range(h):
                o_ref[m, pl.ds(r0, HY_RC), :] = (m_ref[m, pl.ds(r0, HY_RC), :] * d
                                                 + (ic_ref[m, 0] * br - is_ref[m, 0] * bi))
        _row_chunks(body)

    @pl.when(k > 0)
    def _():
        def body(r0):
            br, bi = _cresult(r_ref, r0)
            for m in range(h):
                o_ref[m, pl.ds(r0, HY_RC), :] += ic_ref[m, k] * br - is_ref[m, k] * bi
        _row_chunks(body)


def hy_conv(m_t, kf, d_col, L):
    n, n1, h, nk = _dft_plan(L)
    cs = _dft_consts(L)
    bsz = m_t.shape[0]
    smem = pl.BlockSpec(memory_space=pltpu.SMEM)
    t_spec = pl.BlockSpec((None, h, HY_WIDTH, HY_N2), lambda b, k: (b, 0, 0, 0))
    w_spec = pl.BlockSpec((None, HY_N2, 2 * HY_N2), lambda b, k: (k, 0, 0))
    return pl.pallas_call(
        functools.partial(_hy_conv_kernel, h=h),
        grid=(bsz, nk),
        in_specs=[smem, smem, smem, smem, t_spec,
                  pl.BlockSpec((None, HY_WIDTH, 2 * HY_N2), lambda b, k: (k, 0, 0)),
                  w_spec, w_spec,
                  pl.BlockSpec((HY_WIDTH, 1), lambda b, k: (0, 0))],
        out_specs=t_spec,
        out_shape=jax.ShapeDtypeStruct(m_t.shape, F32),
        scratch_shapes=_hy_scratch(1),
        compiler_params=_cparams(("parallel", "arbitrary"), 52),
        name="hy_conv",
    )(cs["c1"], cs["s1"], cs["ic"], cs["isn"], m_t, kf, cs["wf"], cs["wi"], d_col)


def hyena_mix(p, lp):
    L = p.shape[1]
    hf_t, hb_t, nrm = hy_filter_taps(L, lp)
    kf = hy_spectrum(hf_t, hb_t, (1.0 / nrm).reshape(HY_WIDTH, 1), L)
    m_t, x0 = hy_pre(p, lp["hy_conv_w"], lp["hy_conv_b"])
    z_t = hy_conv(m_t, kf, lp["hy_d"].astype(F32).reshape(HY_WIDTH, 1), L)
    return z_t, x0


def _gelu_tanh(x):
    return 0.5 * x * (1.0 + jnp.tanh(math.sqrt(2.0 / math.pi) * (x + 0.044715 * (x * x * x))))


def _merge_kernel(x_ref, y5_ref, zt_ref, x0_ref,
                  sx_ref, sxp_ref, sxn_ref, sb_ref, sc_ref, scp_ref, scn_ref,
                  g0a_ref, g0b_ref, g1a_ref, g1b_ref, g2a_ref, g2b_ref, mg_ref,
                  glu_ref, scw_ref, sco_ref, hyo_ref, ow_ref, o_ref):
    i = pl.program_id(1)
    first = i == 0
    last = i == pl.num_programs(1) - 1
    ag = _bdot(_gelu_tanh(y5_ref[...]), glu_ref[...])
    y_s5 = ag[:, :D_MODEL] * jax.nn.sigmoid(ag[:, D_MODEL:])
    l7 = slice(SUBLANES - 1, SUBLANES)
    cx = sc_ref[...] * sx_ref[...]
    cxp = scp_ref[l7, :] * sxp_ref[l7, :]
    cxn = scn_ref[0:1, :] * sxn_ref[0:1, :]
    y_sc = _bdot(sb_ref[...] * _conv3(cx, cxp, cxn, scw_ref[...], first, last), sco_ref[...])
    y_hy = _bdot(x0_ref[...] * zt_ref[...].T, hyo_ref[...])
    gate = lambda a, b: jax.nn.sigmoid(jnp.concatenate([a[...], b[...]], axis=1))
    m = gate(g0a_ref, g0b_ref) * y_s5 + gate(g1a_ref, g1b_ref) * y_sc + gate(g2a_ref, g2b_ref) * y_hy
    o_ref[...] = x_ref[...] + mg_ref[...] * _bdot(m, ow_ref[...])


def merge(x, p, y5, z_t, x0, mod3, wl, ctx):
    bsz, L, _ = x.shape
    tl = ROW_TILE
    row = lambda width, col: pl.BlockSpec((None, tl, width), lambda b, i: (b, i, col))
    full = lambda shape: pl.BlockSpec(shape, lambda b, i: (0,) * len(shape))
    c_sc = OFF_SC // SC_WIDTH
    half = D_MODEL // 2
    c_g = OFF_GATE // half
    specs = [row(D_MODEL, 0), row(S5_WIDTH, 0),
             pl.BlockSpec((None, None, HY_WIDTH, HY_N2), lambda b, i: (b, i, 0, 0)), row(HY_WIDTH, 0)]
    specs += _halo_specs(tl, L, SC_WIDTH, c_sc)
    specs += [row(SC_WIDTH, c_sc + 1)]
    specs += _halo_specs(tl, L, SC_WIDTH, c_sc + 2)
    specs += [row(half, c_g + k) for k in range(2 * N_BRANCH)] + [_mod_spec(2, ctx)]
    specs += [full((S5_WIDTH, 2 * D_MODEL)), full((3, SC_WIDTH)), full((SC_WIDTH, D_MODEL)),
              full((HY_WIDTH, D_MODEL)), full((D_MODEL, D_MODEL))]
    return pl.pallas_call(
        _merge_kernel,
        grid=(bsz, L // tl),
        in_specs=specs,
        out_specs=row(D_MODEL, 0),
        out_shape=jax.ShapeDtypeStruct((bsz, L, D_MODEL), F32),
        compiler_params=_cparams(("parallel", "parallel"), 48),
        name="merge",
    )(x, y5, z_t, x0, *([p] * 13), mod3,
      wl["s5_glu_w"], wl["sc_conv_w"], wl["sc_out_w"], wl["hy_out_w"], wl["out_w"])


def _mlp_kernel(x_ref, g_ref, sh_ref, sc_ref, mg_ref, w1_ref, w2_ref, fg_ref, o_ref, h_ref, acc_ref,
                *, final_norm):
    j = pl.program_id(2)

    @pl.when(j == 0)
    def _():
        h_ref[...] = _norm_mod(x_ref[...], g_ref[...], sh_ref[...], sc_ref[...]).astype(BF16)
        acc_ref[...] = jnp.zeros_like(acc_ref)

    r = jnp.maximum(jnp.dot(h_ref[...], w1_ref[...], preferred_element_type=F32), 0.0)
    acc_ref[...] += _bdot(r * r, w2_ref[...])

    @pl.when(j == pl.num_programs(2) - 1)
    def _():
        y = x_ref[...] + mg_ref[...] * acc_ref[...]
        if final_norm:
            y = (y * lax.rsqrt(jnp.mean(y * y, axis=-1, keepdims=True) + EPS)) * fg_ref[...]
        o_ref[...] = y


def mlp(x, g, mod3, w1, w2, final_g, ctx, final_norm):
    bsz, L, _ = x.shape
    tl = min(L, 1024)
    tf = 1024
    x_spec = pl.BlockSpec((None, tl, D_MODEL), lambda b, i, j: (b, i, 0))
    vec = pl.BlockSpec((1, D_MODEL), lambda b, i, j: (0, 0))
    return pl.pallas_call(
        functools.partial(_mlp_kernel, final_norm=final_norm),
        grid=(bsz, L // tl, D_FF // tf),
        in_specs=[x_spec, vec, _mod_spec(3, ctx), _mod_spec(4, ctx), _mod_spec(5, ctx),
                  pl.BlockSpec((D_MODEL, tf), lambda b, i, j: (0, j)),
                  pl.BlockSpec((tf, D_MODEL), lambda b, i, j: (j, 0)),
                  vec],
        out_specs=x_spec,
        out_shape=jax.ShapeDtypeStruct((bsz, L, D_MODEL), F32),
        scratch_shapes=[pltpu.VMEM((tl, D_MODEL), BF16), pltpu.VMEM((tl, D_MODEL), F32)],
        compiler_params=_cparams(("parallel", "parallel", "arbitrary"), 48),
        name="mlp",
    )(x, g.reshape(1, D_MODEL), mod3, mod3, mod3, w1, w2, final_g.reshape(1, D_MODEL))


def _sincos_2d(rows, cols, dim):
    quarter = dim // 4
    omega = 1.0 / (10000.0 ** (jnp.arange(quarter, dtype=F32) / quarter))
    er = jnp.arange(rows, dtype=F32)[:, None] * omega[None]
    ec = jnp.arange(cols, dtype=F32)[:, None] * omega[None]
    er = jnp.concatenate([jnp.sin(er), jnp.cos(er)], axis=-1)
    ec = jnp.concatenate([jnp.sin(ec), jnp.cos(ec)], axis=-1)
    emb = jnp.concatenate([
        jnp.broadcast_to(er[:, None, :], (rows, cols, dim // 2)),
        jnp.broadcast_to(ec[None, :, :], (rows, cols, dim // 2))], axis=-1)
    return emb.reshape(rows * cols, dim)


def kernel(x, c, ctx, c_ctx, ada_w, ada_b, norm1_g, norm2_g, w_in, s5_a_re, s5_a_im, s5_log_dt, s5_b_re, s5_b_im, s5_c_re, s5_c_im, s5_d, s5_glu_w, sc_conv_w, sc_out_w, hy_conv_w, hy_conv_b, hy_f_w1, hy_f_b1, hy_f_w2, hy_f_b2, hy_f_w3, hy_f_b3, hy_f_freq, hy_f_w4, hy_d, hy_out_w, out_w, mlp_w1, mlp_w2, final_g):
    bsz, L, _ = x.shape
    x = x + _sincos_2d(L // GRID_W, GRID_W, D_MODEL)[None]
    xc = ctx
    cc = jnp.concatenate([c, c_ctx[None, :], jnp.zeros((MOD_ROWS - bsz - 1, D_MODEL), F32)], axis=0)
    mod_all = ada_mod(cc, ada_w, ada_b)
    for l in range(DEPTH):
        last = l == DEPTH - 1
        mod3 = mod_all[l].reshape(MOD_ROWS, 1, N_MOD * D_MODEL)
        w_in_l = w_in[l].astype(BF16)
        wl = {"s5_glu_w": s5_glu_w[l].astype(BF16), "sc_conv_w": sc_conv_w[l],
              "sc_out_w": sc_out_w[l].astype(BF16), "hy_out_w": hy_out_w[l].astype(BF16),
              "out_w": out_w[l].astype(BF16)}
        lp = {"hy_conv_w": hy_conv_w[l], "hy_conv_b": hy_conv_b[l],
              "hy_f_w1": hy_f_w1[l], "hy_f_b1": hy_f_b1[l], "hy_f_w2": hy_f_w2[l], "hy_f_b2": hy_f_b2[l],
              "hy_f_w3": hy_f_w3[l], "hy_f_b3": hy_f_b3[l], "hy_f_freq": hy_f_freq[l],
              "hy_f_w4": hy_f_w4[l], "hy_d": hy_d[l]}
        w1 = mlp_w1[l].astype(BF16)
        w2 = mlp_w2[l].astype(BF16)
        ops = _s5_operators(s5_a_re[l], s5_a_im[l], s5_log_dt[l], s5_b_re[l], s5_b_im[l],
                            s5_c_re[l], s5_c_im[l], s5_d[l])

        pc = norm_mm(xc, norm1_g[l], mod3, w_in_l[:, :S5_WIDTH] if last else w_in_l, ctx=True)
        yc5, ctx_final = s5_layer(pc[..., OFF_S5:OFF_S5 + S5_WIDTH], ops, None, not last)

        p = norm_mm(x, norm1_g[l], mod3, w_in_l, ctx=False)
        y5, _ = s5_layer(p[..., OFF_S5:OFF_S5 + S5_WIDTH], ops, ctx_final, True)
        x = merge(x, p, y5, *hyena_mix(p, lp), mod3, wl, ctx=False)
        x = mlp(x, norm2_g[l], mod3, w1, w2, final_g, ctx=False, final_norm=last)

        if not last:
            xc = merge(xc, pc, yc5, *hyena_mix(pc, lp), mod3, wl, ctx=True)
            xc = mlp(xc, norm2_g[l], mod3, w1, w2, final_g, ctx=True, final_norm=False)
    return x
```

```python
import functools
import math

import numpy as np
import jax
import jax.numpy as jnp
from jax import lax
from jax.experimental import pallas as pl
from jax.experimental.pallas import tpu as pltpu

F32 = jnp.float32
BF16 = jnp.bfloat16
HIGHEST = lax.Precision.HIGHEST

D_MODEL = 1024
DEPTH = 2
GRID_W = 64
S5_WIDTH = 512
S5_GROUP = 16
S5_GROUPS = S5_WIDTH // S5_GROUP
S5_STATE = 64
SC_WIDTH = 512
HY_WIDTH = 512
HY_BANDS = 16
HY_EMB = 1 + 2 * HY_BANDS
HY_FAST_DECAY = 0.3
HY_SLOW_DECAY = 1.5
HY_DECAY_TARGET = 1e-2
HY_DECAY_SHIFT = 0.05
HY_MAX_DECAY = math.log(HY_DECAY_TARGET) / HY_FAST_DECAY
HY_MIN_DECAY = math.log(HY_DECAY_TARGET) / HY_SLOW_DECAY
N_BRANCH = 3
D_FF = 4 * D_MODEL
N_MOD = 6
EPS = 1e-6
OFF_S5 = 0
OFF_SC = OFF_S5 + S5_WIDTH
OFF_HY = OFF_SC + 3 * SC_WIDTH
OFF_GATE = OFF_HY + 3 * HY_WIDTH
D_IN = OFF_GATE + N_BRANCH * D_MODEL

LANES = 128
SUBLANES = 8
S5_T = 8
S5_BLOCKS = S5_WIDTH // LANES
S5_HALF = 256
EMB_PAD = 40
MOD_ROWS = 8
CTX_ROW = 4
HY_N2 = 256
ROW_TILE = 256


def _cparams(sem, vmem_mb):
    return pltpu.CompilerParams(dimension_semantics=sem, vmem_limit_bytes=vmem_mb * 1024 * 1024)


def _bdot(a, b):
    return jnp.dot(a.astype(BF16), b.astype(BF16), preferred_element_type=F32)


def _hdot(a, b):
    return jnp.dot(a, b, precision=HIGHEST, preferred_element_type=F32)


def _ada_kernel(c_ref, w_ref, b_ref, o_ref):
    c = c_ref[...]
    s = c * jax.nn.sigmoid(c)
    o_ref[...] = _bdot(s, w_ref[...]) + b_ref[...]


def ada_mod(cc, ada_w, ada_b):
    n = N_MOD * D_MODEL
    tn = 1536
    return pl.pallas_call(
        _ada_kernel,
        grid=(DEPTH, n // tn),
        in_specs=[
            pl.BlockSpec((MOD_ROWS, D_MODEL), lambda l, j: (0, 0)),
            pl.BlockSpec((None, D_MODEL, tn), lambda l, j: (l, 0, j)),
            pl.BlockSpec((None, 1, tn), lambda l, j: (l, 0, j)),
        ],
        out_specs=pl.BlockSpec((None, MOD_ROWS, tn), lambda l, j: (l, 0, j)),
        out_shape=jax.ShapeDtypeStruct((DEPTH, MOD_ROWS, n), F32),
        compiler_params=_cparams(("parallel", "parallel"), 40),
        name="ada_mod",
    )(cc, ada_w, ada_b.reshape(DEPTH, 1, n))


def _mod_spec(chunk, ctx):
    if ctx:
        return pl.BlockSpec((None, 1, D_MODEL), lambda b, *_: (CTX_ROW, 0, chunk))
    return pl.BlockSpec((None, 1, D_MODEL), lambda b, *_: (b, 0, chunk))


def _norm_mod(x, g, sh, sc):
    y = x * lax.rsqrt(jnp.mean(x * x, axis=-1, keepdims=True) + EPS)
    return (y * g) * (1.0 + sc) + sh


def _norm_mm_kernel(x_ref, g_ref, sh_ref, sc_ref, w_ref, o_ref, h_ref):
    @pl.when(pl.program_id(2) == 0)
    def _():
        h_ref[...] = _norm_mod(x_ref[...], g_ref[...], sh_ref[...], sc_ref[...]).astype(BF16)

    o_ref[...] = jnp.dot(h_ref[...], w_ref[...], preferred_element_type=F32)


def norm_mm(x, g, mod3, w, ctx):
    bsz, L, _ = x.shape
    n = w.shape[1]
    tl = min(L, 1024)
    tn = n // 4 if n % (4 * LANES) == 0 and n > 2048 else n
    return pl.pallas_call(
        _norm_mm_kernel,
        grid=(bsz, L // tl, n // tn),
        in_specs=[
            pl.BlockSpec((None, tl, D_MODEL), lambda b, i, j: (b, i, 0)),
            pl.BlockSpec((1, D_MODEL), lambda b, i, j: (0, 0)),
            _mod_spec(0, ctx),
            _mod_spec(1, ctx),
            pl.BlockSpec((D_MODEL, tn), lambda b, i, j: (0, j)),
        ],
        out_specs=pl.BlockSpec((None, tl, tn), lambda b, i, j: (b, i, j)),
        out_shape=jax.ShapeDtypeStruct((bsz, L, n), F32),
        scratch_shapes=[pltpu.VMEM((tl, D_MODEL), BF16)],
        compiler_params=_cparams(("parallel", "parallel", "arbitrary"), 48),
        name="norm_mm",
    )(x, g.reshape(1, D_MODEL), mod3, mod3, w)


@functools.lru_cache(maxsize=None)
def _s5_spread_consts():
    T, P = S5_T, S5_GROUP
    src = np.arange(T * P)
    dst = np.arange(T * LANES)
    spread = (src[:, None] // P == dst[None, :] // LANES) & (src[:, None] % P == dst[None, :] % P)
    grp = (dst // P) % 8
    same = grp[:, None] == grp[None, :]
    return spread.astype(np.float32).astype(BF16), same.astype(np.float32).astype(BF16)


def _s5_operators(a_re, a_im, log_dt, b_re, b_im, c_re, c_im, d_skip):
    T = S5_T
    G, N, P = S5_GROUPS, S5_STATE, S5_GROUP
    taus = jnp.arange(T + 1, dtype=F32)[:, None, None]
    ops = []
    for k in range(2):
        ar, ai = a_re[k].astype(F32), a_im[k].astype(F32)
        dt = jnp.exp(log_dt[k].astype(F32))[:, None]
        mag = jnp.exp(ar * dt * taus)
        ang = ai * dt * taus
        pr, pi = mag * jnp.cos(ang), mag * jnp.sin(ang)
        den = ar * ar + ai * ai
        nr, ni = pr[1] - 1.0, pi[1]
        f_re = (nr * ar + ni * ai) / den
        f_im = (ni * ar - nr * ai) / den
        br, bi = b_re[k].astype(F32), b_im[k].astype(F32)
        bbr = f_re[..., None] * br - f_im[..., None] * bi
        bbi = f_re[..., None] * bi + f_im[..., None] * br
        er = pr[..., None] * bbr - pi[..., None] * bbi
        ei = pr[..., None] * bbi + pi[..., None] * bbr
        cr, ci = c_re[k].astype(F32), c_im[k].astype(F32)
        kk = (jnp.einsum("gqn,tgnp->tgqp", cr, er[:T], precision=HIGHEST)
              - jnp.einsum("gqn,tgnp->tgqp", ci, ei[:T], precision=HIGHEST))
        car = cr[None] * pr[:, :, None, :] - ci[None] * pi[:, :, None, :]
        cai = cr[None] * pi[:, :, None, :] + ci[None] * pr[:, :, None, :]
        ops.append((pr, pi, er, ei, kk, car, cai))

    j_in = jnp.arange(T)[:, None]
    j_out = jnp.arange(T)[None, :]
    tau = (j_out - j_in)[..., None, None, None]
    dterm = jnp.eye(P, dtype=F32)[None] * d_skip.astype(F32).reshape(G, 1, P)
    ksum = (jnp.where(tau >= 0, ops[0][4][jnp.clip(j_out - j_in, 0, T - 1)], 0.0)
            + jnp.where(tau <= 0, ops[1][4][jnp.clip(j_in - j_out, 0, T - 1)], 0.0)
            + jnp.where(tau == 0, dterm[None, None], 0.0))
    ks = ksum.reshape(T, T, S5_BLOCKS, 8, P, P).transpose(2, 0, 3, 5, 1, 4)
    ks = ks.reshape(S5_BLOCKS, T * LANES, T * P).astype(BF16)
    spread, same_group = _s5_spread_consts()
    m_op = jnp.einsum("brk,kc->brc", ks, spread, preferred_element_type=BF16) * same_group[None]

    lane = jnp.arange(2 * S5_HALF)
    grp_half = (jnp.arange(8)[None, :, None]
                == jnp.arange(2)[:, None, None] * 4 + ((lane % S5_HALF) // N)[None, None, :]).astype(BF16)
    grp_all = jnp.concatenate([grp_half[0], grp_half[1]], axis=-1)
    row_group = (np.arange(T * LANES) // P) % 8

    def b_mat(er, ei, sel):
        def one(e):
            e6 = e[sel].reshape(T, S5_BLOCKS, 2, 4, N, P)
            return e6.transpose(1, 2, 0, 5, 3, 4).reshape(S5_BLOCKS, 2, T, P, S5_HALF)
        v = jnp.concatenate([one(er), one(ei)], axis=-1).astype(BF16)
        v = v.reshape(S5_BLOCKS, 2, T * P, 2 * S5_HALF)
        rows = jnp.einsum("rk,bhkc->bhrc", spread.T, v, preferred_element_type=BF16)
        return rows * grp_half[:, row_group, :][None]

    def c_mat_t(car, cai, sel):
        def one(c):
            c6 = c[sel].reshape(T, S5_BLOCKS, 2, 4, P, N)
            return c6.transpose(1, 0, 4, 2, 3, 5).reshape(S5_BLOCKS, T, P, 2, S5_HALF)
        v = jnp.stack([one(car), -one(cai)], axis=4).reshape(S5_BLOCKS, T * P, 4 * S5_HALF).astype(BF16)
        rows = jnp.einsum("rk,bkc->brc", spread.T, v, preferred_element_type=BF16)
        return rows * grp_all[row_group, :][None]

    jj = jnp.arange(T)
    b_f = b_mat(ops[0][2], ops[0][3], T - 1 - jj)
    b_r = b_mat(ops[1][2], ops[1][3], jj)
    c_f = c_mat_t(ops[0][5], ops[0][6], jj + 1)
    c_r = c_mat_t(ops[1][5], ops[1][6], T - jj)

    def at(pr, pi):
        r = pr[T].reshape(S5_BLOCKS, 2, 1, S5_HALF)
        i = pi[T].reshape(S5_BLOCKS, 2, 1, S5_HALF)
        return jnp.concatenate([r, i], axis=-1)

    return dict(m=m_op, b_f=b_f, b_r=b_r, c_f=c_f, c_r=c_r,
                at_f=at(ops[0][0], ops[0][1]), at_r=at(ops[1][0], ops[1][1]))


def _cmul_add(a, s, d):
    h = S5_HALF
    ar, ai = a[:, :h], a[:, h:]
    sr, si = s[:, :h], s[:, h:]
    return jnp.concatenate([ar * sr - ai * si, ar * si + ai * sr], axis=1) + d


def _s5_scan_kernel(x_ref, bf_ref, br_ref, atf_ref, atr_ref, if_ref, ir_ref,
                    s_ref, r_ref, ff_ref, fr_ref, *, rows):
    rc = min(rows, 256)
    for r0 in range(0, rows, rc):
        xs = x_ref[r0:r0 + rc, :]
        s_ref[r0:r0 + rc, :] = jnp.dot(xs, bf_ref[...], preferred_element_type=F32)
        r_ref[r0:r0 + rc, :] = jnp.dot(xs, br_ref[...], preferred_element_type=F32)

    ntile = rows // SUBLANES
    atf = atf_ref[...]
    atr = atr_ref[...]
    low = lax.broadcasted_iota(jnp.int32, (SUBLANES, 2 * S5_HALF), 0) < 4

    def body(i, carry):
        s, r = carry
        fo = pl.multiple_of(i * SUBLANES, SUBLANES)
        d = s_ref[pl.ds(fo, SUBLANES), :]
        t1 = pltpu.roll(_cmul_add(atf, s, d), 4, 0)
        t2 = _cmul_add(atf, t1, d)
        s_ref[pl.ds(fo, SUBLANES), :] = jnp.where(low, s, t1)
        ro = pl.multiple_of((ntile - 1 - i) * SUBLANES, SUBLANES)
        e = r_ref[pl.ds(ro, SUBLANES), :]
        u1 = pltpu.roll(_cmul_add(atr, r, e), 4, 0)
        u2 = _cmul_add(atr, u1, e)
        r_ref[pl.ds(ro, SUBLANES), :] = jnp.where(low, u1, r)
        return pltpu.roll(t2, 4, 0), pltpu.roll(u2, 4, 0)

    s0 = if_ref[...]
    r0 = pltpu.roll(ir_ref[...], 4, 0)
    s, r = lax.fori_loop(0, ntile, body, (s0, r0), unroll=2)
    ff_ref[...] = s
    fr_ref[...] = pltpu.roll(r, 4, 0)


def s5_scan(x4, ops, init_f, init_r):
    nb, rows, tk = x4.shape
    w = 2 * S5_HALF
    kern = functools.partial(_s5_scan_kernel, rows=rows)
    st_spec = pl.BlockSpec((None, None, SUBLANES, w), lambda b, h: (b, h, 0, 0))
    out_shape = (
        jax.ShapeDtypeStruct((nb, rows, 2 * w), F32),
        jax.ShapeDtypeStruct((nb, rows, 2 * w), F32),
        jax.ShapeDtypeStruct((nb, 2, SUBLANES, w), F32),
        jax.ShapeDtypeStruct((nb, 2, SUBLANES, w), F32),
    )
    return pl.pallas_call(
        kern,
        grid=(nb, 2),
        in_specs=[
            pl.BlockSpec((None, rows, tk), lambda b, h: (b, 0, 0)),
            pl.BlockSpec((None, None, tk, w), lambda b, h: (b, h, 0, 0)),
            pl.BlockSpec((None, None, tk, w), lambda b, h: (b, h, 0, 0)),
            pl.BlockSpec((None, None, 1, w), lambda b, h: (b, h, 0, 0)),
            pl.BlockSpec((None, None, 1, w), lambda b, h: (b, h, 0, 0)),
            st_spec, st_spec,
        ],
        out_specs=(
            pl.BlockSpec((None, rows, w), lambda b, h: (b, 0, h)),
            pl.BlockSpec((None, rows, w), lambda b, h: (b, 0, h)),
            st_spec, st_spec,
        ),
        out_shape=out_shape,
        compiler_params=_cparams(("parallel", "parallel"), 48),
        name="s5_scan",
    )(x4, ops["b_f"], ops["b_r"], ops["at_f"], ops["at_r"], init_f, init_r)


def _s5_out_kernel(x_ref, s_ref, r_ref, m_ref, cf_ref, cr_ref, o_ref):
    nt = (((1,), (1,)), ((), ()))
    acc = jnp.dot(x_ref[...], m_ref[...], preferred_element_type=F32)
    acc += lax.dot_general(s_ref[...].astype(BF16), cf_ref[...], nt, preferred_element_type=F32)
    acc += lax.dot_general(r_ref[...].astype(BF16), cr_ref[...], nt, preferred_element_type=F32)
    o_ref[...] = acc


def s5_out(x4, s_st, r_st, ops):
    nb, rows, tk = x4.shape
    tr = min(rows, 512)
    w = 4 * S5_HALF
    return pl.pallas_call(
        _s5_out_kernel,
        grid=(nb, rows // tr),
        in_specs=[
            pl.BlockSpec((None, tr, tk), lambda b, i: (b, i, 0)),
            pl.BlockSpec((None, tr, w), lambda b, i: (b, i, 0)),
            pl.BlockSpec((None, tr, w), lambda b, i: (b, i, 0)),
            pl.BlockSpec((None, tk, tk), lambda b, i: (b, 0, 0)),
            pl.BlockSpec((None, tk, w), lambda b, i: (b, 0, 0)),
            pl.BlockSpec((None, tk, w), lambda b, i: (b, 0, 0)),
        ],
        out_specs=pl.BlockSpec((None, tr, tk), lambda b, i: (b, i, 0)),
        out_shape=jax.ShapeDtypeStruct((nb, rows, tk), F32),
        compiler_params=_cparams(("parallel", "parallel"), 48),
        name="s5_out",
    )(x4, s_st, r_st, ops["m"], ops["c_f"], ops["c_r"])


def _to_chunk_rows(u):
    bsz, L, _ = u.shape
    nc = L // S5_T
    x = u.astype(BF16).reshape(bsz, nc, S5_T, S5_BLOCKS, LANES)
    return x.transpose(3, 1, 0, 2, 4).reshape(S5_BLOCKS, nc * bsz, S5_T * LANES)


def _from_chunk_rows(y4, bsz):
    nb, rows, _ = y4.shape
    nc = rows // bsz
    y = y4.reshape(nb, nc, bsz, S5_T, LANES).transpose(2, 1, 3, 0, 4)
    return y.reshape(bsz, nc * S5_T, S5_WIDTH)


def s5_layer(u, ops, init, readout):
    bsz = u.shape[0]
    x4 = _to_chunk_rows(u)
    if init is None:
        z = jnp.zeros((S5_BLOCKS, 2, SUBLANES, 2 * S5_HALF), F32)
        init = (z, z)
    s_st, r_st, fin_f, fin_r = s5_scan(x4, ops, init[0], init[1])
    y = _from_chunk_rows(s5_out(x4, s_st, r_st, ops), bsz) if readout else None
    return y, (fin_f, fin_r)


def _conv3(x, prev_row, next_row, w, first, last):
    n = x.shape[0]
    rows = lax.broadcasted_iota(jnp.int32, x.shape, 0)
    pz = jnp.where(first, 0.0, prev_row)
    nz = jnp.where(last, 0.0, next_row)
    xp = jnp.where(rows == 0, pz, pltpu.roll(x, 1, 0))
    xn = jnp.where(rows == n - 1, nz, pltpu.roll(x, n - 1, 0))
    return w[0:1, :] * xp + w[1:2, :] * x + w[2:3, :] * xn


def _halo_specs(tl, L, width, col):
    tb = tl // SUBLANES
    nb = L // SUBLANES
    cur = pl.BlockSpec((None, tl, width), lambda b, i: (b, i, col))
    prv = pl.BlockSpec((None, SUBLANES, width), lambda b, i: (b, jnp.maximum(i * tb - 1, 0), col))
    nxt = pl.BlockSpec((None, SUBLANES, width), lambda b, i: (b, jnp.minimum((i + 1) * tb, nb - 1), col))
    return [cur, prv, nxt]


def _dft_plan(L):
    n = 2 * L
    n1 = n // HY_N2
    return n, n1, n1 // 2, n1 // 2 + 1


@functools.lru_cache(maxsize=None)
def _dft_consts(L):
    n, n1, h, nk = _dft_plan(L)
    k1 = np.arange(nk)[:, None]
    m1 = np.arange(h)[None, :]
    th = 2.0 * np.pi * k1 * m1 / n1
    wt = np.where((k1 == 0) | (k1 == h), 1.0, 2.0) / n
    k2 = np.arange(HY_N2)[:, None]
    m2 = np.arange(HY_N2)[None, :]
    ph = -2.0 * np.pi * (k2 * m2 / HY_N2)[None] - 2.0 * np.pi * np.arange(nk)[:, None, None] * m2[None] / n
    gr, gi = np.cos(ph), np.sin(ph)
    wf = np.concatenate([gr.transpose(0, 2, 1), gi.transpose(0, 2, 1)], axis=2)
    wi = np.concatenate([gr, -gi], axis=2)
    f = lambda a: np.asarray(a, np.float32)
    return dict(c1=f(np.cos(th)), s1=f(-np.sin(th)),
                ic=f((wt * np.cos(th)).T), isn=f((wt * np.sin(th)).T),
                wf=f(wf).astype(BF16), wi=f(wi).astype(BF16))


HY_RC = 32
W = HY_WIDTH


def _row_chunks(body):
    def step(ci, carry):
        body(pl.multiple_of(ci * HY_RC, HY_RC))
        return carry
    lax.fori_loop(0, W // HY_RC, step, 0)


def _store_operand(a_ref, r0, vr, vi):
    a_ref[pl.ds(r0, HY_RC), :] = vr.astype(BF16)
    a_ref[pl.ds(W + r0, HY_RC), :] = vi.astype(BF16)


def _stage1(x_ref, c_ref, s_ref, k, h, a_ref):
    def body(r0):
        x0 = x_ref[0, pl.ds(r0, HY_RC), :]
        ar = c_ref[k, 0] * x0
        ai = s_ref[k, 0] * x0
        for m in range(1, h):
            xm = x_ref[m, pl.ds(r0, HY_RC), :]
            ar += c_ref[k, m] * xm
            ai += s_ref[k, m] * xm
        _store_operand(a_ref, r0, ar, ai)
    _row_chunks(body)


def _cmatmul(a_ref, w, r_ref):
    r_ref[...] = jnp.dot(a_ref[...], w, preferred_element_type=F32)


def _cresult(r_ref, r0):
    top = r_ref[pl.ds(r0, HY_RC), :]
    bot = r_ref[pl.ds(W + r0, HY_RC), :]
    return top[:, :HY_N2] - bot[:, HY_N2:], top[:, HY_N2:] + bot[:, :HY_N2]


def _hy_pre_kernel(v_ref, vp_ref, vn_ref, a_ref, ap_ref, an_ref, z_ref, zp_ref, zn_ref,
                   w_ref, b_ref, mt_ref, x0_ref):
    i = pl.program_id(1)
    first = i == 0
    last = i == pl.num_programs(1) - 1
    w = w_ref[...]
    b = b_ref[...]

    def part(c, p, n, k):
        sl = slice(k * HY_WIDTH, (k + 1) * HY_WIDTH)
        return _conv3(c[...], p[SUBLANES - 1:SUBLANES, :], n[0:1, :], w[:, sl], first, last) + b[:, sl]

    v = part(v_ref, vp_ref, vn_ref, 0)
    x1 = part(a_ref, ap_ref, an_ref, 1)
    mt_ref[...] = (x1 * v).T
    x0_ref[...] = part(z_ref, zp_ref, zn_ref, 2)


def hy_pre(p, conv_w, conv_b):
    bsz, L, _ = p.shape
    tl = ROW_TILE
    c0 = OFF_HY // HY_WIDTH
    specs = []
    for k in range(3):
        specs += _halo_specs(tl, L, HY_WIDTH, c0 + k)
    specs += [pl.BlockSpec((3, 3 * HY_WIDTH), lambda b, i: (0, 0)),
              pl.BlockSpec((1, 3 * HY_WIDTH), lambda b, i: (0, 0))]
    return pl.pallas_call(
        _hy_pre_kernel,
        grid=(bsz, L // tl),
        in_specs=specs,
        out_specs=(pl.BlockSpec((None, None, HY_WIDTH, HY_N2), lambda b, i: (b, i, 0, 0)),
                   pl.BlockSpec((None, tl, HY_WIDTH), lambda b, i: (b, i, 0))),
        out_shape=(jax.ShapeDtypeStruct((bsz, L // HY_N2, HY_WIDTH, HY_N2), F32),
                   jax.ShapeDtypeStruct((bsz, L, HY_WIDTH), F32)),
        compiler_params=_cparams(("parallel", "parallel"), 32),
        name="hy_pre",
    )(*([p] * 9), conv_w, conv_b.reshape(1, -1))


def _hy_filter_kernel(z_ref, w1_ref, b1_ref, w2_ref, b2_ref, w3_ref, b3_ref, fr_ref, w4_ref,
                      dl_ref, hf_ref, hb_ref, nrm_ref, *, tl):
    i = pl.program_id(0)
    z = z_ref[...]
    fr = fr_ref[...]
    h = jnp.sin(fr * (_hdot(z, w1_ref[...]) + b1_ref[...]))
    h = jnp.sin(fr * (_hdot(h, w2_ref[...]) + b2_ref[...]))
    h = jnp.sin(fr * (_hdot(h, w3_ref[...]) + b3_ref[...]))
    h = _hdot(h, w4_ref[...])
    t = z[:, 0:1]
    window = jnp.exp(-t * dl_ref[...]) + HY_DECAY_SHIFT
    hf = h[:, :HY_WIDTH] * window
    rows = lax.broadcasted_iota(jnp.int32, (tl, HY_WIDTH), 0) + i * tl
    hb = jnp.where(rows == 0, 0.0, h[:, HY_WIDTH:] * window)
    hf_ref[...] = hf.T
    hb_ref[...] = hb.T
    part = jnp.sum(jnp.abs(hf) + jnp.abs(hb), axis=0, keepdims=True)

    @pl.when(i == 0)
    def _():
        nrm_ref[...] = jnp.zeros_like(nrm_ref)

    nrm_ref[...] += part


def hy_filter_taps(L, lp):
    t = jnp.linspace(0.0, 1.0, L, dtype=F32)[:, None]
    ang = 2.0 * math.pi * jnp.arange(L, dtype=F32)[:, None] / L
    bands = jnp.linspace(1e-4, HY_BANDS - 1, HY_BANDS, dtype=F32)[None, :]
    z = jnp.concatenate([t, jnp.cos(bands * ang), -jnp.sin(bands * ang),
                         jnp.zeros((L, EMB_PAD - HY_EMB), F32)], axis=-1)
    w1 = jnp.concatenate([lp["hy_f_w1"], jnp.zeros((EMB_PAD - HY_EMB, lp["hy_f_w1"].shape[1]), F32)], axis=0)
    deltas = jnp.abs(jnp.linspace(HY_MIN_DECAY, HY_MAX_DECAY, HY_WIDTH, dtype=F32))[None, :]
    tl = HY_N2
    hid = w1.shape[1]
    full = lambda shape: pl.BlockSpec(shape, lambda i: (0,) * len(shape))
    o_spec = pl.BlockSpec((None, HY_WIDTH, HY_N2), lambda i: (i, 0, 0))
    shp = jax.ShapeDtypeStruct((L // HY_N2, HY_WIDTH, HY_N2), F32)
    return pl.pallas_call(
        functools.partial(_hy_filter_kernel, tl=tl),
        grid=(L // tl,),
        in_specs=[pl.BlockSpec((tl, EMB_PAD), lambda i: (i, 0)),
                  full((EMB_PAD, hid)), full((1, hid)), full((hid, hid)), full((1, hid)),
                  full((hid, hid)), full((1, hid)), full((1, hid)), full((hid, 2 * HY_WIDTH)),
                  full((1, HY_WIDTH))],
        out_specs=(o_spec, o_spec, pl.BlockSpec((1, HY_WIDTH), lambda i: (0, 0))),
        out_shape=(shp, shp, jax.ShapeDtypeStruct((1, HY_WIDTH), F32)),
        compiler_params=_cparams(("arbitrary",), 32),
        name="hy_filter_taps",
    )(z, w1, lp["hy_f_b1"].reshape(1, -1), lp["hy_f_w2"], lp["hy_f_b2"].reshape(1, -1),
      lp["hy_f_w3"], lp["hy_f_b3"].reshape(1, -1), lp["hy_f_freq"].reshape(1, -1), lp["hy_f_w4"], deltas)


def _hy_spectrum_kernel(c_ref, s_ref, hf_ref, hb_ref, w_ref, inv_ref, k_ref,
                        a_ref, rf_ref, rb_ref, *, h):
    k = pl.program_id(0)
    _stage1(hf_ref, c_ref, s_ref, k, h, a_ref)
    _cmatmul(a_ref, w_ref[...], rf_ref)
    _stage1(hb_ref, c_ref, s_ref, k, h, a_ref)
    _cmatmul(a_ref, w_ref[...], rb_ref)

    def body(r0):
        fr, fi = _cresult(rf_ref, r0)
        br, bi = _cresult(rb_ref, r0)
        inv = inv_ref[pl.ds(r0, HY_RC), :]
        k_ref[pl.ds(r0, HY_RC), :] = jnp.concatenate([(fr + br) * inv, (fi - bi) * inv], axis=1)
    _row_chunks(body)


def hy_spectrum(hf_t, hb_t, inv_norm, L):
    n, n1, h, nk = _dft_plan(L)
    cs = _dft_consts(L)
    smem = pl.BlockSpec(memory_space=pltpu.SMEM)
    t_spec = pl.BlockSpec((h, HY_WIDTH, HY_N2), lambda k: (0, 0, 0))
    w_spec = pl.BlockSpec((None, HY_N2, 2 * HY_N2), lambda k: (k, 0, 0))
    return pl.pallas_call(
        functools.partial(_hy_spectrum_kernel, h=h),
        grid=(nk,),
        in_specs=[smem, smem, t_spec, t_spec, w_spec,
                  pl.BlockSpec((HY_WIDTH, 1), lambda k: (0, 0))],
        out_specs=pl.BlockSpec((None, HY_WIDTH, 2 * HY_N2), lambda k: (k, 0, 0)),
        out_shape=jax.ShapeDtypeStruct((nk, HY_WIDTH, 2 * HY_N2), F32),
        scratch_shapes=_hy_scratch(2),
        compiler_params=_cparams(("parallel",), 48),
        name="hy_spectrum",
    )(cs["c1"], cs["s1"], hf_t, hb_t, cs["wf"], inv_norm)


def _hy_scratch(n_results):
    return ([pltpu.VMEM((2 * W, HY_N2), BF16)]
            + [pltpu.VMEM((2 * W, 2 * HY_N2), F32)] * n_results)


def _hy_conv_kernel(c_ref, s_ref, ic_ref, is_ref, m_ref, kf_ref, wf_ref, wi_ref,
                    d_ref, o_ref, a_ref, r_ref, *, h):
    k = pl.program_id(1)
    _stage1(m_ref, c_ref, s_ref, k, h, a_ref)
    _cmatmul(a_ref, wf_ref[...], r_ref)

    def spectrum_product(r0):
        yr, yi = _cresult(r_ref, r0)
        kf = kf_ref[pl.ds(r0, HY_RC), :]
        kr, ki = kf[:, :HY_N2], kf[:, HY_N2:]
        _store_operand(a_ref, r0, yr * kr - yi * ki, yr * ki + yi * kr)
    _row_chunks(spectrum_product)
    _cmatmul(a_ref, wi_ref[...], r_ref)

    @pl.when(k == 0)
    def _():
        def body(r0):
            br, bi = _cresult(r_ref, r0)
            d = d_ref[pl.ds(r0, HY_RC), :]
            for m in range(h):
                o_ref[m, pl.ds(r0, HY_RC), :] = (m_ref[m, pl.ds(r0, HY_RC), :] * d
                                                 + (ic_ref[m, 0] * br - is_ref[m, 0] * bi))
        _row_chunks(body)

    @pl.when(k > 0)
    def _():
        def body(r0):
            br, bi = _cresult(r_ref, r0)
            for m in range(h):
                o_ref[m, pl.ds(r0, HY_RC), :] += ic_ref[m, k] * br - is_ref[m, k] * bi
        _row_chunks(body)


def hy_conv(m_t, kf, d_col, L):
    n, n1, h, nk = _dft_plan(L)
    cs = _dft_consts(L)
    bsz = m_t.shape[0]
    smem = pl.BlockSpec(memory_space=pltpu.SMEM)
    t_spec = pl.BlockSpec((None, h, HY_WIDTH, HY_N2), lambda b, k: (b, 0, 0, 0))
    w_spec = pl.BlockSpec((None, HY_N2, 2 * HY_N2), lambda b, k: (k, 0, 0))
    return pl.pallas_call(
        functools.partial(_hy_conv_kernel, h=h),
        grid=(bsz, nk),
        in_specs=[smem, smem, smem, smem, t_spec,
                  pl.BlockSpec((None, HY_WIDTH, 2 * HY_N2), lambda b, k: (k, 0, 0)),
                  w_spec, w_spec,
                  pl.BlockSpec((HY_WIDTH, 1), lambda b, k: (0, 0))],
        out_specs=t_spec,
        out_shape=jax.ShapeDtypeStruct(m_t.shape, F32),
        scratch_shapes=_hy_scratch(1),
        compiler_params=_cparams(("parallel", "arbitrary"), 52),
        name="hy_conv",
    )(cs["c1"], cs["s1"], cs["ic"], cs["isn"], m_t, kf, cs["wf"], cs["wi"], d_col)


def hyena_mix(p, lp):
    L = p.shape[1]
    hf_t, hb_t, nrm = hy_filter_taps(L, lp)
    kf = hy_spectrum(hf_t, hb_t, (1.0 / nrm).reshape(HY_WIDTH, 1), L)
    m_t, x0 = hy_pre(p, lp["hy_conv_w"], lp["hy_conv_b"])
    z_t = hy_conv(m_t, kf, lp["hy_d"].astype(F32).reshape(HY_WIDTH, 1), L)
    return z_t, x0


def _gelu_tanh(x):
    return 0.5 * x * (1.0 + jnp.tanh(math.sqrt(2.0 / math.pi) * (x + 0.044715 * (x * x * x))))


def _merge_kernel(x_ref, y5_ref, zt_ref, x0_ref,
                  sx_ref, sxp_ref, sxn_ref, sb_ref, sc_ref, scp_ref, scn_ref,
                  g0a_ref, g0b_ref, g1a_ref, g1b_ref, g2a_ref, g2b_ref, mg_ref,
                  glu_ref, scw_ref, sco_ref, hyo_ref, ow_ref, o_ref):
    i = pl.program_id(1)
    first = i == 0
    last = i == pl.num_programs(1) - 1
    ag = _bdot(_gelu_tanh(y5_ref[...]), glu_ref[...])
    y_s5 = ag[:, :D_MODEL] * jax.nn.sigmoid(ag[:, D_MODEL:])
    l7 = slice(SUBLANES - 1, SUBLANES)
    cx = sc_ref[...] * sx_ref[...]
    cxp = scp_ref[l7, :] * sxp_ref[l7, :]
    cxn = scn_ref[0:1, :] * sxn_ref[0:1, :]
    y_sc = _bdot(sb_ref[...] * _conv3(cx, cxp, cxn, scw_ref[...], first, last), sco_ref[...])
    y_hy = _bdot(x0_ref[...] * zt_ref[...].T, hyo_ref[...])
    gate = lambda a, b: jax.nn.sigmoid(jnp.concatenate([a[...], b[...]], axis=1))
    m = gate(g0a_ref, g0b_ref) * y_s5 + gate(g1a_ref, g1b_ref) * y_sc + gate(g2a_ref, g2b_ref) * y_hy
    o_ref[...] = x_ref[...] + mg_ref[...] * _bdot(m, ow_ref[...])


def merge(x, p, y5, z_t, x0, mod3, wl, ctx):
    bsz, L, _ = x.shape
    tl = ROW_TILE
    row = lambda width, col: pl.BlockSpec((None, tl, width), lambda b, i: (b, i, col))
    full = lambda shape: pl.BlockSpec(shape, lambda b, i: (0,) * len(shape))
    c_sc = OFF_SC // SC_WIDTH
    half = D_MODEL // 2
    c_g = OFF_GATE // half
    specs = [row(D_MODEL, 0), row(S5_WIDTH, 0),
             pl.BlockSpec((None, None, HY_WIDTH, HY_N2), lambda b, i: (b, i, 0, 0)), row(HY_WIDTH, 0)]
    specs += _halo_specs(tl, L, SC_WIDTH, c_sc)
    specs += [row(SC_WIDTH, c_sc + 1)]
    specs += _halo_specs(tl, L, SC_WIDTH, c_sc + 2)
    specs += [row(half, c_g + k) for k in range(2 * N_BRANCH)] + [_mod_spec(2, ctx)]
    specs += [full((S5_WIDTH, 2 * D_MODEL)), full((3, SC_WIDTH)), full((SC_WIDTH, D_MODEL)),
              full((HY_WIDTH, D_MODEL)), full((D_MODEL, D_MODEL))]
    return pl.pallas_call(
        _merge_kernel,
        grid=(bsz, L // tl),
        in_specs=specs,
        out_specs=row(D_MODEL, 0),
        out_shape=jax.ShapeDtypeStruct((bsz, L, D_MODEL), F32),
        compiler_params=_cparams(("parallel", "parallel"), 48),
        name="merge",
    )(x, y5, z_t, x0, *([p] * 13), mod3,
      wl["s5_glu_w"], wl["sc_conv_w"], wl["sc_out_w"], wl["hy_out_w"], wl["out_w"])


def _mlp_kernel(x_ref, g_ref, sh_ref, sc_ref, mg_ref, w1_ref, w2_ref, fg_ref, o_ref, h_ref, acc_ref,
                *, final_norm):
    j = pl.program_id(2)

    @pl.when(j == 0)
    def _():
        h_ref[...] = _norm_mod(x_ref[...], g_ref[...], sh_ref[...], sc_ref[...]).astype(BF16)
        acc_ref[...] = jnp.zeros_like(acc_ref)

    r = jnp.maximum(jnp.dot(h_ref[...], w1_ref[...], preferred_element_type=F32), 0.0)
    acc_ref[...] += _bdot(r * r, w2_ref[...])

    @pl.when(j == pl.num_programs(2) - 1)
    def _():
        y = x_ref[...] + mg_ref[...] * acc_ref[...]
        if final_norm:
            y = (y * lax.rsqrt(jnp.mean(y * y, axis=-1, keepdims=True) + EPS)) * fg_ref[...]
        o_ref[...] = y


def mlp(x, g, mod3, w1, w2, final_g, ctx, final_norm):
    bsz, L, _ = x.shape
    tl = min(L, 1024)
    tf = 1024
    x_spec = pl.BlockSpec((None, tl, D_MODEL), lambda b, i, j: (b, i, 0))
    vec = pl.BlockSpec((1, D_MODEL), lambda b, i, j: (0, 0))
    return pl.pallas_call(
        functools.partial(_mlp_kernel, final_norm=final_norm),
        grid=(bsz, L // tl, D_FF // tf),
        in_specs=[x_spec, vec, _mod_spec(3, ctx), _mod_spec(4, ctx), _mod_spec(5, ctx),
                  pl.BlockSpec((D_MODEL, tf), lambda b, i, j: (0, j)),
                  pl.BlockSpec((tf, D_MODEL), lambda b, i, j: (j, 0)),
                  vec],
        out_specs=x_spec,
        out_shape=jax.ShapeDtypeStruct((bsz, L, D_MODEL), F32),
        scratch_shapes=[pltpu.VMEM((tl, D_MODEL), BF16), pltpu.VMEM((tl, D_MODEL), F32)],
        compiler_params=_cparams(("parallel", "parallel", "arbitrary"), 48),
        name="mlp",
    )(x, g.reshape(1, D_MODEL), mod3, mod3, mod3, w1, w2, final_g.reshape(1, D_MODEL))


def _sincos_2d(rows, cols, dim):
    quarter = dim // 4
    omega = 1.0 / (10000.0 ** (jnp.arange(quarter, dtype=F32) / quarter))
    er = jnp.arange(rows, dtype=F32)[:, None] * omega[None]
    ec = jnp.arange(cols, dtype=F32)[:, None] * omega[None]
    er = jnp.concatenate([jnp.sin(er), jnp.cos(er)], axis=-1)
    ec = jnp.concatenate([jnp.sin(ec), jnp.cos(ec)], axis=-1)
    emb = jnp.concatenate([
        jnp.broadcast_to(er[:, None, :], (rows, cols, dim // 2)),
        jnp.broadcast_to(ec[None, :, :], (rows, cols, dim // 2))], axis=-1)
    return emb.reshape(rows * cols, dim)


def kernel(x, c, ctx, c_ctx, ada_w, ada_b, norm1_g, norm2_g, w_in, s5_a_re, s5_a_im, s5_log_dt, s5_b_re, s5_b_im, s5_c_re, s5_c_im, s5_d, s5_glu_w, sc_conv_w, sc_out_w, hy_conv_w, hy_conv_b, hy_f_w1, hy_f_b1, hy_f_w2, hy_f_b2, hy_f_w3, hy_f_b3, hy_f_freq, hy_f_w4, hy_d, hy_out_w, out_w, mlp_w1, mlp_w2, final_g):
    bsz, L, _ = x.shape
    x = x + _sincos_2d(L // GRID_W, GRID_W, D_MODEL)[None]
    xc = ctx
    cc = jnp.concatenate([c, c_ctx[None, :], jnp.zeros((MOD_ROWS - bsz - 1, D_MODEL), F32)], axis=0)
    mod_all = ada_mod(cc, ada_w, ada_b)
    for l in range(DEPTH):
        last = l == DEPTH - 1
        mod3 = mod_all[l].reshape(MOD_ROWS, 1, N_MOD * D_MODEL)
        w_in_l = w_in[l].astype(BF16)
        wl = {"s5_glu_w": s5_glu_w[l].astype(BF16), "sc_conv_w": sc_conv_w[l],
              "sc_out_w": sc_out_w[l].astype(BF16), "hy_out_w": hy_out_w[l].astype(BF16),
              "out_w": out_w[l].astype(BF16)}
        lp = {"hy_conv_w": hy_conv_w[l], "hy_conv_b": hy_conv_b[l],
              "hy_f_w1": hy_f_w1[l], "hy_f_b1": hy_f_b1[l], "hy_f_w2": hy_f_w2[l], "hy_f_b2": hy_f_b2[l],
              "hy_f_w3": hy_f_w3[l], "hy_f_b3": hy_f_b3[l], "hy_f_freq": hy_f_freq[l],
              "hy_f_w4": hy_f_w4[l], "hy_d": hy_d[l]}
        w1 = mlp_w1[l].astype(BF16)
        w2 = mlp_w2[l].astype(BF16)
        ops = _s5_operators(s5_a_re[l], s5_a_im[l], s5_log_dt[l], s5_b_re[l], s5_b_im[l],
                            s5_c_re[l], s5_c_im[l], s5_d[l])

        pc = norm_mm(xc, norm1_g[l], mod3, w_in_l[:, :S5_WIDTH] if last else w_in_l, ctx=True)
        yc5, ctx_final = s5_layer(pc[..., OFF_S5:OFF_S5 + S5_WIDTH], ops, None, not last)

        p = norm_mm(x, norm1_g[l], mod3, w_in_l, ctx=False)
        y5, _ = s5_layer(p[..., OFF_S5:OFF_S5 + S5_WIDTH], ops, ctx_final, True)
        x = merge(x, p, y5, *hyena_mix(p, lp), mod3, wl, ctx=False)
        x = mlp(x, norm2_g[l], mod3, w1, w2, final_g, ctx=False, final_norm=last)

        if not last:
            xc = merge(xc, pc, yc5, *hyena_mix(pc, lp), mod3, wl, ctx=True)
            xc = mlp(xc, norm2_g[l], mod3, w1, w2, final_g, ctx=True, final_norm=False)
    return x
```
